```python
import jax, jax.numpy as jnp
from jax import lax
import numpy as np

D_MODEL = 1024
BATCH = 8
SEQ = 4096
DEPTH = 4

N_A_LAYERS = DEPTH // 2
N_B_LAYERS = DEPTH - N_A_LAYERS
POOL_WINDOWS = (2, 4, 8, 16)
N_POOL_GROUPS = len(POOL_WINDOWS)
POOL_GROUP = D_MODEL // N_POOL_GROUPS
N_HEADS = 8
QK_NOPE = 128
QK_ROPE = 64
V_HEAD = 128
QK_HEAD = QK_NOPE + QK_ROPE
Q_RANK = 3 * D_MODEL // 8
KV_RANK = D_MODEL // 4
ROPE_THETA = 10000.0
Q_BLOCK = 128
D_FF = ((8 * D_MODEL // 3 + 127) // 128) * 128
CONV_WIDTH = 3
EPS = 1e-6
N_MOD = 6
MAX_POS_OFFSET = 1024

kernel_name = "yoco_pool_mla_adaln_convglu"


def rmsnorm(x, g):
    x32 = x.astype(jnp.float32)
    y = x32 * lax.rsqrt(jnp.mean(x32 * x32, axis=-1, keepdims=True) + EPS)
    return y.astype(x.dtype) * g


def modulate(h, shift, scale):
    return h * (1 + scale[:, None, :]) + shift[:, None, :]


def rope_tables(positions):
    inv = 1.0 / (ROPE_THETA ** (jnp.arange(0, QK_ROPE, 2, dtype=jnp.float32) / QK_ROPE))
    ang = positions.astype(jnp.float32)[..., None] * inv
    return jnp.cos(ang), jnp.sin(ang)


def apply_rope(x, cos, sin):
    x32 = x.astype(jnp.float32)
    x1, x2 = jnp.split(x32, 2, axis=-1)
    out = jnp.concatenate([x1 * cos - x2 * sin, x2 * cos + x1 * sin], axis=-1)
    return out.astype(x.dtype)


def trailing_mean_minus_self(h, w):
    s = h.shape[1]
    h32 = h.astype(jnp.float32)
    cs = jnp.cumsum(h32, axis=1)
    cs_lag = jnp.pad(cs, ((0, 0), (w, 0), (0, 0)))[:, :s]
    count = jnp.minimum(jnp.arange(1, s + 1, dtype=jnp.float32), float(w))
    mean = (cs - cs_lag) / count[None, :, None]
    return (mean - h32).astype(h.dtype)


def pool_mixer(h, w_g, b_g, scale):
    bsz, s, d = h.shape
    hg = h.reshape(bsz, s, N_POOL_GROUPS, POOL_GROUP)
    pooled = jnp.stack([trailing_mean_minus_self(hg[:, :, g], POOL_WINDOWS[g])
                        for g in range(N_POOL_GROUPS)], axis=2)
    y = jnp.einsum('bsgc,gcd->bsgd', pooled, w_g).reshape(bsz, s, d) + b_g
    return y * scale


def conv_glu_ffn(h, w_up, conv_w, conv_b, w_down):
    s = h.shape[1]
    a, v = jnp.split(h @ w_up, 2, axis=-1)
    ap = jnp.pad(a, ((0, 0), (CONV_WIDTH - 1, 0), (0, 0)))
    a = sum(ap[:, k:k + s] * conv_w[k] for k in range(CONV_WIDTH)) + conv_b
    return (jax.nn.gelu(a, approximate=False) * v) @ w_down


def shared_kv(x, kv_in_g, w_dkv, ckv_norm_g, w_uk, w_uv, cos, sin):
    bsz, s, _ = x.shape
    kv = rmsnorm(x, kv_in_g) @ w_dkv
    c_kv = rmsnorm(kv[..., :KV_RANK], ckv_norm_g)
    k_rope = apply_rope(kv[..., KV_RANK:], cos, sin)
    k_nope = (c_kv @ w_uk).reshape(bsz, s, N_HEADS, QK_NOPE)
    k = jnp.concatenate([k_nope, jnp.broadcast_to(k_rope[:, :, None, :],
                                                  (bsz, s, N_HEADS, QK_ROPE))], axis=-1)
    v = (c_kv @ w_uv).reshape(bsz, s, N_HEADS, V_HEAD)
    return k, v


def causal_block_attention(q, k, v):
    s = q.shape[1]
    q = q * (QK_HEAD ** -0.5)
    outs = []
    for i in range(s // Q_BLOCK):
        q0 = i * Q_BLOCK
        k_end = q0 + Q_BLOCK
        sc = jnp.einsum('bqhd,bkhd->bhqk', q[:, q0:k_end], k[:, :k_end],
                        preferred_element_type=jnp.float32)
        mask = jnp.arange(k_end)[None, :] <= jnp.arange(q0, k_end)[:, None]
        sc = jnp.where(mask, sc, jnp.finfo(jnp.float32).min)
        p = jax.nn.softmax(sc, axis=-1).astype(v.dtype)
        outs.append(jnp.einsum('bhqk,bkhd->bqhd', p, v[:, :k_end]))
    return jnp.concatenate(outs, axis=1)


def mla_mixer(h, k, v, w_dq, q_norm_g, w_uq, w_o, cos, sin):
    bsz, s, _ = h.shape
    c_q = rmsnorm(h @ w_dq, q_norm_g)
    q = (c_q @ w_uq).reshape(bsz, s, N_HEADS, QK_HEAD)
    q = jnp.concatenate([q[..., :QK_NOPE],
                         apply_rope(q[..., QK_NOPE:], cos[:, :, None], sin[:, :, None])], axis=-1)
    o = causal_block_attention(q, k, v)
    return o.reshape(bsz, s, N_HEADS * V_HEAD) @ w_o


def setup_inputs(seed: int = 0) -> dict:
    key = jax.random.key(seed)
    ks = jax.random.split(key, 26)
    f32 = jnp.float32
    nrm = lambda k, shape, s: jax.random.normal(k, shape, f32) * s
    d, f = D_MODEL, D_FF
    positions = (jnp.arange(SEQ, dtype=jnp.int32)[None, :]
                 + jax.random.randint(ks[2], (BATCH, 1), 0, MAX_POS_OFFSET, dtype=jnp.int32))
    return {
        "x": nrm(ks[0], (BATCH, SEQ, d), 1.0),
        "c": nrm(ks[1], (BATCH, d), 1.0),
        "positions": positions,
        "mod_w": nrm(ks[3], (DEPTH, d, N_MOD * d), d ** -0.5),
        "mod_b": nrm(ks[4], (DEPTH, N_MOD * d), 0.01),
        "norm1_g": 1.0 + nrm(ks[5], (DEPTH, d), 0.02),
        "norm2_g": 1.0 + nrm(ks[6], (DEPTH, d), 0.02),
        "pool_w": nrm(ks[7], (N_A_LAYERS, N_POOL_GROUPS, POOL_GROUP, POOL_GROUP), POOL_GROUP ** -0.5),
        "pool_b": nrm(ks[8], (N_A_LAYERS, d), 0.01),
        "pool_scale": 1.0 + nrm(ks[9], (N_A_LAYERS, d), 0.1),
        "kv_in_g": 1.0 + nrm(ks[10], (d,), 0.02),
        "w_dkv": nrm(ks[11], (d, KV_RANK + QK_ROPE), d ** -0.5),
        "ckv_norm_g": 1.0 + nrm(ks[12], (KV_RANK,), 0.02),
        "w_uk": nrm(ks[13], (KV_RANK, N_HEADS * QK_NOPE), KV_RANK ** -0.5),
        "w_uv": nrm(ks[14], (KV_RANK, N_HEADS * V_HEAD), KV_RANK ** -0.5),
        "w_dq": nrm(ks[15], (N_B_LAYERS, d, Q_RANK), d ** -0.5),
        "q_norm_g": 1.0 + nrm(ks[16], (N_B_LAYERS, Q_RANK), 0.02),
        "w_uq": nrm(ks[17], (N_B_LAYERS, Q_RANK, N_HEADS * QK_HEAD), Q_RANK ** -0.5),
        "w_o": nrm(ks[18], (N_B_LAYERS, N_HEADS * V_HEAD, d), (N_HEADS * V_HEAD) ** -0.5),
        "w_up": nrm(ks[19], (DEPTH, d, 2 * f), d ** -0.5),
        "conv_w": nrm(ks[20], (DEPTH, CONV_WIDTH, f), CONV_WIDTH ** -0.5),
        "conv_b": nrm(ks[21], (DEPTH, f), 0.01),
        "w_down": nrm(ks[22], (DEPTH, f, d), f ** -0.5),
        "final_g": 1.0 + nrm(ks[23], (d,), 0.02),
    }


def reference(x, c, positions, mod_w, mod_b, norm1_g, norm2_g, pool_w, pool_b, pool_scale,
              kv_in_g, w_dkv, ckv_norm_g, w_uk, w_uv, w_dq, q_norm_g, w_uq, w_o,
              w_up, conv_w, conv_b, w_down, final_g):
    cos, sin = rope_tables(positions)
    mods = jnp.einsum('bd,lde->lbe', jax.nn.silu(c), mod_w) + mod_b[:, None, :]
    k = v = None
    for l in range(DEPTH):
        sh1, sc1, g1, sh2, sc2, g2 = jnp.split(mods[l], N_MOD, axis=-1)
        h = modulate(rmsnorm(x, norm1_g[l]), sh1, sc1)
        if l < N_A_LAYERS:
            y = pool_mixer(h, pool_w[l], pool_b[l], pool_scale[l])
        else:
            j = l - N_A_LAYERS
            y = mla_mixer(h, k, v, w_dq[j], q_norm_g[j], w_uq[j], w_o[j], cos, sin)
        x = x + g1[:, None, :] * y
        h = modulate(rmsnorm(x, norm2_g[l]), sh2, sc2)
        x = x + g2[:, None, :] * conv_glu_ffn(h, w_up[l], conv_w[l], conv_b[l], w_down[l])
        if l == N_A_LAYERS - 1:
            k, v = shared_kv(x, kv_in_g, w_dkv, ckv_norm_g, w_uk, w_uv, cos, sin)
    return rmsnorm(x, final_g)
```

```python
import functools
import math

import jax
import jax.numpy as jnp
from jax import lax
from jax.experimental import pallas as pl
from jax.experimental.pallas import tpu as pltpu

F32 = jnp.float32
BF16 = jnp.bfloat16

POOL_WINDOWS = (2, 4, 8, 16)
N_HEADS = 8
QK_NOPE = 128
QK_ROPE = 64
V_HEAD = 128
QK_HEAD = QK_NOPE + QK_ROPE
ROPE_THETA = 10000.0
CONV_WIDTH = 3
EPS = 1e-6
N_MOD = 6

LANES = 128
SUBLANES = 8
VMEM_LIMIT_BYTES = 56 * 1024 * 1024

SEQ_TILE = 512
FF_CHUNK = 256
ATTN_TILE = 512
POOL_HALO = 16
MODS_N_TILE = 1536


def _params(sem):
    return pltpu.CompilerParams(dimension_semantics=sem, vmem_limit_bytes=VMEM_LIMIT_BYTES)


def _rmsnorm(x, g):
    ms = jnp.mean(x * x, axis=-1, keepdims=True)
    return x * lax.rsqrt(ms + EPS) * g


def _dot(a, b):
    return jnp.dot(a, b, preferred_element_type=F32)


def _rope_table_kernel(pos_ref, inv_ref, sign_ref, cc_ref, ss_ref):
    ang = pos_ref[0] * inv_ref[...]
    cc_ref[0] = jnp.cos(ang)
    ss_ref[0] = jnp.sin(ang) * sign_ref[...]


def _rope_tables(positions):
    b, s = positions.shape
    half = QK_ROPE // 2
    inv = 1.0 / (ROPE_THETA ** (jnp.arange(0, QK_ROPE, 2, dtype=F32) / QK_ROPE))
    inv128 = jnp.tile(inv, LANES // half)[None, :]
    sign128 = jnp.tile(jnp.concatenate([-jnp.ones((half,), F32), jnp.ones((half,), F32)]),
                       LANES // QK_ROPE)[None, :]
    pos = positions.astype(F32)[..., None]
    ts = SEQ_TILE
    vec = pl.BlockSpec((1, LANES), lambda i, j: (0, 0))
    tab = pl.BlockSpec((1, ts, LANES), lambda i, j: (i, j, 0))
    return pl.pallas_call(
        _rope_table_kernel,
        grid=(b, s // ts),
        in_specs=[pl.BlockSpec((1, ts, 1), lambda i, j: (i, j, 0)), vec, vec],
        out_specs=[tab, tab],
        out_shape=[jax.ShapeDtypeStruct((b, s, LANES), F32)] * 2,
        compiler_params=_params(("parallel", "parallel")),
        name="rope_tables",
    )(pos, inv128, sign128)


def _mods_kernel(c_ref, w_ref, b_ref, o_ref):
    c = c_ref[...]
    sc = (c * jax.nn.sigmoid(c)).astype(BF16)
    o_ref[0] = _dot(sc, w_ref[0].astype(BF16)) + b_ref[0]


def _mods(c, mod_w, mod_b):
    depth, d, n = mod_w.shape
    b = c.shape[0]
    nt = MODS_N_TILE
    return pl.pallas_call(
        _mods_kernel,
        grid=(depth, n // nt),
        in_specs=[pl.BlockSpec((b, d), lambda l, j: (0, 0)),
                  pl.BlockSpec((1, d, nt), lambda l, j: (l, 0, j)),
                  pl.BlockSpec((1, 1, nt), lambda l, j: (l, 0, j))],
        out_specs=pl.BlockSpec((1, b, nt), lambda l, j: (l, 0, j)),
        out_shape=jax.ShapeDtypeStruct((depth, b, n), F32),
        compiler_params=_params(("parallel", "parallel")),
        name="adaln_mods",
    )(c, mod_w, mod_b.reshape(depth, 1, n))


def _pool_kernel(x_ref, mod_ref, g_ref, w_ref, pb_ref, ps_ref, o_ref, hbuf):
    s = pl.program_id(1)
    ts = x_ref.shape[1]
    d = x_ref.shape[2]
    group = d // len(POOL_WINDOWS)

    @pl.when(s == 0)
    def _():
        hbuf[0:POOL_HALO, :] = jnp.zeros((POOL_HALO, d), F32)

    x = x_ref[0]
    mod = mod_ref[0]
    h = _rmsnorm(x, g_ref[...]) * (1.0 + mod[1:2]) + mod[0:1]
    hbuf[POOL_HALO:POOL_HALO + ts, :] = h
    t = s * ts + lax.broadcasted_iota(jnp.int32, (ts, 1), 0)
    ys = []
    for gi, w in enumerate(POOL_WINDOWS):
        lo = gi * group
        hg = h[:, lo:lo + group]
        acc = hg
        for k in range(1, w):
            acc = acc + hbuf[POOL_HALO - k:POOL_HALO - k + ts, lo:lo + group]
        cnt = jnp.minimum(t + 1, w).astype(F32)
        pooled = acc / cnt - hg
        ys.append(_dot(pooled.astype(BF16), w_ref[gi]))
    y = (jnp.concatenate(ys, axis=-1) + pb_ref[...]) * ps_ref[...]
    o_ref[0] = x + mod[2:3] * y
    hbuf[0:POOL_HALO, :] = hbuf[ts:ts + POOL_HALO, :]


def _pool_layer(x, mods_l, g, w, pb, ps):
    b, s, d = x.shape
    ts = SEQ_TILE
    vec = pl.BlockSpec((1, d), lambda i, j: (0, 0))
    tile = pl.BlockSpec((1, ts, d), lambda i, j: (i, j, 0))
    return pl.pallas_call(
        _pool_kernel,
        grid=(b, s // ts),
        in_specs=[tile,
                  pl.BlockSpec((1, N_MOD, d), lambda i, j: (i, 0, 0)),
                  vec,
                  pl.BlockSpec(w.shape, lambda i, j: (0, 0, 0)),
                  vec, vec],
        out_specs=tile,
        out_shape=jax.ShapeDtypeStruct(x.shape, F32),
        scratch_shapes=[pltpu.VMEM((ts + POOL_HALO, d), F32)],
        compiler_params=_params(("arbitrary", "arbitrary")),
        name="pool_mixer",
    )(x, mods_l, g, w, pb, ps)


def _shift_rows(a, k, prev_rows):
    rolled = pltpu.roll(a, k, axis=0)
    prev = pltpu.roll(prev_rows, k, axis=0)
    row = lax.broadcasted_iota(jnp.int32, prev_rows.shape, 0)
    head = jnp.where(row < k, prev, rolled[0:SUBLANES])
    return jnp.concatenate([head, rolled[SUBLANES:]], axis=0)


def _ffn_kernel(*refs, with_attn, with_final):
    it = iter(refs)
    x_ref = next(it)
    mod_ref = next(it)
    if with_attn:
        o_ref_in = next(it)
        wo_ref = next(it)
    g_ref = next(it)
    wa_ref = next(it)
    wv_ref = next(it)
    cw_ref = next(it)
    cb_ref = next(it)
    wd_ref = next(it)
    if with_final:
        fg_ref = next(it)
    out_ref = next(it)
    carry_ref = next(it)
    acc_ref = next(it)

    s = pl.program_id(1)
    ts = x_ref.shape[1]
    n_chunks = wa_ref.shape[0]

    @pl.when(s == 0)
    def _():
        carry_ref[...] = jnp.zeros(carry_ref.shape, F32)

    x = x_ref[0]
    mod = mod_ref[0]
    if with_attn:
        x = x + mod[2:3] * _dot(o_ref_in[0], wo_ref[...])
    h = (_rmsnorm(x, g_ref[...]) * (1.0 + mod[4:5]) + mod[3:4]).astype(BF16)

    sqrt_half = math.sqrt(0.5)
    for j in range(n_chunks):
        a = _dot(h, wa_ref[j])
        v = _dot(h, wv_ref[j])
        prev = carry_ref[j]
        cw = cw_ref[j]
        a1 = _shift_rows(a, 1, prev)
        a2 = _shift_rows(a, 2, prev)
        conv = a2 * cw[0:1] + a1 * cw[1:2] + a * cw[2:3] + cb_ref[j]
        gate = 0.5 * conv * (1.0 + lax.erf(conv * sqrt_half))
        contrib = _dot((gate * v).astype(BF16), wd_ref[j])
        if j == 0:
            acc_ref[...] = contrib
        else:
            acc_ref[...] += contrib
        carry_ref[j] = a[ts - SUBLANES:ts]

    y = x + mod[5:6] * acc_ref[...]
    if with_final:
        y = _rmsnorm(y, fg_ref[...])
    out_ref[0] = y


def _ffn_layer(x, mods_l, g, wa, wv, cw, cb, wd, attn=None, final_g=None):
    b, s, d = x.shape
    ts = SEQ_TILE
    n_chunks, _, fc = wa.shape
    const2 = lambda i, j: (0, 0)
    const3 = lambda i, j: (0, 0, 0)
    vec = pl.BlockSpec((1, d), const2)
    tile = pl.BlockSpec((1, ts, d), lambda i, j: (i, j, 0))
    single = dict(pipeline_mode=pl.Buffered(1))
    in_specs = [tile, pl.BlockSpec((1, N_MOD, d), lambda i, j: (i, 0, 0))]
    args = [x, mods_l]
    if attn is not None:
        o, wo = attn
        in_specs += [pl.BlockSpec((1, ts, o.shape[2]), lambda i, j: (i, j, 0)),
                     pl.BlockSpec(wo.shape, const2, **single)]
        args += [o, wo]
    in_specs += [vec,
                 pl.BlockSpec(wa.shape, const3, **single),
                 pl.BlockSpec(wv.shape, const3, **single),
                 pl.BlockSpec(cw.shape, const3),
                 pl.BlockSpec(cb.shape, const3),
                 pl.BlockSpec(wd.shape, const3, **single)]
    args += [g, wa, wv, cw, cb, wd]
    if final_g is not None:
        in_specs.append(vec)
        args.append(final_g)
    kern = functools.partial(_ffn_kernel, with_attn=attn is not None, with_final=final_g is not None)
    return pl.pallas_call(
        kern,
        grid=(b, s // ts),
        in_specs=in_specs,
        out_specs=tile,
        out_shape=jax.ShapeDtypeStruct(x.shape, F32),
        scratch_shapes=[pltpu.VMEM((n_chunks, SUBLANES, fc), F32),
                        pltpu.VMEM((ts, d), F32)],
        compiler_params=_params(("arbitrary", "arbitrary")),
        name="convglu_ffn",
    )(*args)


def _kv_kernel(x_ref, g_ref, wc_ref, wr_ref, wrs_ref, cg_ref, wuk_ref, wuv_ref, cc_ref, ss_ref,
               k_ref, v_ref):
    xn = _rmsnorm(x_ref[0], g_ref[...]).astype(BF16)
    c_kv = _rmsnorm(_dot(xn, wc_ref[...]), cg_ref[...]).astype(BF16)
    cc = cc_ref[0][:, 0:QK_ROPE]
    ss = ss_ref[0][:, 0:QK_ROPE]
    k_rope = (_dot(xn, wr_ref[...]) * cc + _dot(xn, wrs_ref[...]) * ss).astype(BF16)
    k_nope = _dot(c_kv, wuk_ref[...])
    v = _dot(c_kv, wuv_ref[...])
    for hd in range(N_HEADS):
        k_ref[0, hd, :, 0:QK_NOPE] = k_nope[:, hd * QK_NOPE:(hd + 1) * QK_NOPE].astype(BF16)
        k_ref[0, hd, :, QK_NOPE:QK_HEAD] = k_rope
        v_ref[0, hd] = v[:, hd * V_HEAD:(hd + 1) * V_HEAD].astype(BF16)


def _shared_kv(x, g, wc, wr, wrs, cg, wuk, wuv, cc, ss):
    b, s, d = x.shape
    ts = SEQ_TILE
    const2 = lambda i, j: (0, 0)
    full = lambda a: pl.BlockSpec(a.shape, const2)
    tab = pl.BlockSpec((1, ts, LANES), lambda i, j: (i, j, 0))
    return pl.pallas_call(
        _kv_kernel,
        grid=(b, s // ts),
        in_specs=[pl.BlockSpec((1, ts, d), lambda i, j: (i, j, 0)),
                  full(g), full(wc), full(wr), full(wrs), full(cg), full(wuk), full(wuv), tab, tab],
        out_specs=[pl.BlockSpec((1, N_HEADS, ts, QK_HEAD), lambda i, j: (i, 0, j, 0)),
                   pl.BlockSpec((1, N_HEADS, ts, V_HEAD), lambda i, j: (i, 0, j, 0))],
        out_shape=[jax.ShapeDtypeStruct((b, N_HEADS, s, QK_HEAD), BF16),
                   jax.ShapeDtypeStruct((b, N_HEADS, s, V_HEAD), BF16)],
        compiler_params=_params(("parallel", "parallel")),
        name="shared_kv",
    )(x, g, wc, wr, wrs, cg, wuk, wuv, cc, ss)


def _q_kernel(x_ref, mod_ref, g_ref, wdq_ref, qg_ref, wuq_ref, cc_ref, ss_ref, q_ref):
    mod = mod_ref[0]
    h = (_rmsnorm(x_ref[0], g_ref[...]) * (1.0 + mod[1:2]) + mod[0:1]).astype(BF16)
    c_q = _rmsnorm(_dot(h, wdq_ref[...]), qg_ref[...]).astype(BF16)
    width = N_HEADS * LANES
    scale = QK_HEAD ** -0.5
    q_nope = _dot(c_q, wuq_ref[:, 0:width]) * scale
    cc = jnp.concatenate([cc_ref[0]] * N_HEADS, axis=-1)
    ss = jnp.concatenate([ss_ref[0]] * N_HEADS, axis=-1)
    q_rope = (_dot(c_q, wuq_ref[:, width:2 * width]) * cc
              + _dot(c_q, wuq_ref[:, 2 * width:3 * width]) * ss) * scale
    for hd in range(N_HEADS):
        q_ref[0, hd, :, 0:QK_NOPE] = q_nope[:, hd * LANES:(hd + 1) * LANES].astype(BF16)
        q_ref[0, hd, :, QK_NOPE:QK_HEAD] = q_rope[:, hd * LANES:hd * LANES + QK_ROPE].astype(BF16)


def _q_proj(x, mods_l, g, wdq, qg, wuq, cc, ss):
    b, s, d = x.shape
    ts = SEQ_TILE
    const2 = lambda i, j: (0, 0)
    full = lambda a: pl.BlockSpec(a.shape, const2)
    tab = pl.BlockSpec((1, ts, LANES), lambda i, j: (i, j, 0))
    return pl.pallas_call(
        _q_kernel,
        grid=(b, s // ts),
        in_specs=[pl.BlockSpec((1, ts, d), lambda i, j: (i, j, 0)),
                  pl.BlockSpec((1, N_MOD, d), lambda i, j: (i, 0, 0)),
                  full(g), full(wdq), full(qg), full(wuq), tab, tab],
        out_specs=pl.BlockSpec((1, N_HEADS, ts, QK_HEAD), lambda i, j: (i, 0, j, 0)),
        out_shape=jax.ShapeDtypeStruct((b, N_HEADS, s, QK_HEAD), BF16),
        compiler_params=_params(("parallel", "parallel")),
        name="q_proj",
    )(x, mods_l, g, wdq, qg, wuq, cc, ss)


def _attn_kernel(q_ref, k_ref, v_ref, o_ref, m_ref, l_ref, acc_ref):
    qi = pl.program_id(2)
    tq = q_ref.shape[2]
    tk = tq
    q = q_ref[0, 0]
    m_ref[...] = jnp.full(m_ref.shape, -jnp.inf, F32)
    l_ref[...] = jnp.zeros(l_ref.shape, F32)
    acc_ref[...] = jnp.zeros(acc_ref.shape, F32)

    def step(j, masked):
        start = pl.multiple_of(j * tk, tk)
        k = k_ref[0, 0, pl.ds(start, tk), :]
        v = v_ref[0, 0, pl.ds(start, tk), :]
        sc = lax.dot_general(q, k, (((1,), (1,)), ((), ())), preferred_element_type=F32)
        if masked:
            row = lax.broadcasted_iota(jnp.int32, sc.shape, 0)
            col = lax.broadcasted_iota(jnp.int32, sc.shape, 1)
            sc = jnp.where(col <= row, sc, jnp.finfo(F32).min)
        m_prev = m_ref[...]
        m_new = jnp.maximum(m_prev, jnp.max(sc, axis=-1, keepdims=True))
        alpha = jnp.exp(m_prev - m_new)
        p = jnp.exp(sc - m_new)
        l_ref[...] = alpha * l_ref[...] + jnp.sum(p, axis=-1, keepdims=True)
        acc_ref[...] = alpha * acc_ref[...] + _dot(p.astype(BF16), v)
        m_ref[...] = m_new

    def body(j, carry):
        step(j, False)
        return carry

    lax.fori_loop(0, qi, body, 0)
    step(qi, True)
    o_ref[0] = (acc_ref[...] / l_ref[...]).astype(o_ref.dtype)


def _attention(q, k, v):
    b, nh, s, dk = q.shape
    dv = v.shape[3]
    tq = ATTN_TILE
    return pl.pallas_call(
        _attn_kernel,
        grid=(b, nh, s // tq),
        in_specs=[pl.BlockSpec((1, 1, tq, dk), lambda i, h, j: (i, h, j, 0)),
                  pl.BlockSpec((1, 1, s, dk), lambda i, h, j: (i, h, 0, 0)),
                  pl.BlockSpec((1, 1, s, dv), lambda i, h, j: (i, h, 0, 0))],
        out_specs=pl.BlockSpec((1, tq, dv), lambda i, h, j: (i, j, h)),
        out_shape=jax.ShapeDtypeStruct((b, s, nh * dv), BF16),
        scratch_shapes=[pltpu.VMEM((tq, 1), F32), pltpu.VMEM((tq, 1), F32),
                        pltpu.VMEM((tq, dv), F32)],
        compiler_params=_params(("parallel", "parallel", "arbitrary")),
        name="causal_attention",
    )(q, k, v)


def _swap_rope_halves(w):
    half = QK_ROPE // 2
    return jnp.concatenate([w[..., half:], w[..., :half]], axis=-1)


def _prep_ffn(w_up, conv_w, conv_b, w_down):
    d, two_f = w_up.shape
    f = two_f // 2
    n_chunks = f // FF_CHUNK
    wu = w_up.astype(BF16)
    wa = wu[:, :f].reshape(d, n_chunks, FF_CHUNK).transpose(1, 0, 2)
    wv = wu[:, f:].reshape(d, n_chunks, FF_CHUNK).transpose(1, 0, 2)
    cw = conv_w.reshape(CONV_WIDTH, n_chunks, FF_CHUNK).transpose(1, 0, 2)
    cb = conv_b.reshape(n_chunks, 1, FF_CHUNK)
    wd = w_down.astype(BF16).reshape(n_chunks, FF_CHUNK, d)
    return wa, wv, cw, cb, wd


def _prep_wuq(w_uq):
    r = w_uq.shape[0]
    w = w_uq.reshape(r, N_HEADS, QK_HEAD)
    nope = w[:, :, :QK_NOPE].reshape(r, N_HEADS * QK_NOPE)
    rope = w[:, :, QK_NOPE:]
    pad = jnp.zeros((r, N_HEADS, LANES - QK_ROPE), w_uq.dtype)
    rope_p = jnp.concatenate([rope, pad], axis=-1).reshape(r, N_HEADS * LANES)
    rope_s = jnp.concatenate([_swap_rope_halves(rope), pad], axis=-1).reshape(r, N_HEADS * LANES)
    return jnp.concatenate([nope, rope_p, rope_s], axis=-1).astype(BF16)


def kernel(x, c, positions, mod_w, mod_b, norm1_g, norm2_g, pool_w, pool_b, pool_scale, kv_in_g,
           w_dkv, ckv_norm_g, w_uk, w_uv, w_dq, q_norm_g, w_uq, w_o, w_up, conv_w, conv_b, w_down,
           final_g):
    b, s, d = x.shape
    depth = mod_w.shape[0]
    n_pool = pool_w.shape[0]
    kv_rank = ckv_norm_g.shape[0]

    cc, ss = _rope_tables(positions)
    mods = _mods(c, mod_w, mod_b).reshape(depth, b, N_MOD, d)
    row = lambda a: a.reshape(1, -1)

    k = v = None
    for l in range(depth):
        ffn_w = _prep_ffn(w_up[l], conv_w[l], conv_b[l], w_down[l])
        final = row(final_g) if l == depth - 1 else None
        if l < n_pool:
            x = _pool_layer(x, mods[l], row(norm1_g[l]), pool_w[l].astype(BF16), row(pool_b[l]),
                            row(pool_scale[l]))
            x = _ffn_layer(x, mods[l], row(norm2_g[l]), *ffn_w, final_g=final)
        else:
            j = l - n_pool
            q = _q_proj(x, mods[l], row(norm1_g[l]), w_dq[j].astype(BF16), row(q_norm_g[j]),
                        _prep_wuq(w_uq[j]), cc, ss)
            o = _attention(q, k, v)
            x = _ffn_layer(x, mods[l], row(norm2_g[l]), *ffn_w, attn=(o, w_o[j].astype(BF16)),
                           final_g=final)
        if l == n_pool - 1:
            w_rope = w_dkv[:, kv_rank:]
            k, v = _shared_kv(x, row(kv_in_g), w_dkv[:, :kv_rank].astype(BF16), w_rope.astype(BF16),
                              _swap_rope_halves(w_rope).astype(BF16), row(ckv_norm_g),
                              w_uk.astype(BF16), w_uv.astype(BF16), cc, ss)
    return x
```

```python
import functools
import math

import jax
import jax.numpy as jnp
from jax import lax
from jax.experimental import pallas as pl
from jax.experimental.pallas import tpu as pltpu

F32 = jnp.float32
BF16 = jnp.bfloat16

POOL_WINDOWS = (2, 4, 8, 16)
N_HEADS = 8
QK_NOPE = 128
QK_ROPE = 64
V_HEAD = 128
QK_HEAD = QK_NOPE + QK_ROPE
ROPE_THETA = 10000.0
CONV_WIDTH = 3
EPS = 1e-6
N_MOD = 6

LANES = 128
SUBLANES = 8
VMEM_LIMIT_BYTES = 56 * 1024 * 1024

SEQ_TILE = 512
FF_CHUNK = 256
ATTN_TILE = 1024
ATTN_KEY_CHUNK = 256
POOL_HALO = 16
MODS_N_TILE = 1536


def _params(sem, flags=None):
    return pltpu.CompilerParams(dimension_semantics=sem, vmem_limit_bytes=VMEM_LIMIT_BYTES,
                                flags=flags)


def _rmsnorm(x, g):
    ms = jnp.mean(x * x, axis=-1, keepdims=True)
    return x * lax.rsqrt(ms + EPS) * g


def _dot(a, b):
    return jnp.dot(a, b, preferred_element_type=F32)


def _rope_table_kernel(pos_ref, inv_ref, cc_ref, ss_ref, cct_ref, sst_ref):
    ang = inv_ref[...] * pos_ref[0]
    cos_t = jnp.cos(ang)
    sin_t = jnp.sin(ang)
    cct = jnp.concatenate([cos_t, cos_t], axis=0)
    sst = jnp.concatenate([-sin_t, sin_t], axis=0)
    cct_ref[0] = cct
    sst_ref[0] = sst
    cc_ref[0] = jnp.concatenate([cct, cct], axis=0).T
    ss_ref[0] = jnp.concatenate([sst, sst], axis=0).T


def _rope_tables(positions):
    b, s = positions.shape
    half = QK_ROPE // 2
    inv = 1.0 / (ROPE_THETA ** (jnp.arange(0, QK_ROPE, 2, dtype=F32) / QK_ROPE))
    pos = positions.astype(F32)[:, None, :]
    ts = SEQ_TILE
    tab = pl.BlockSpec((1, ts, LANES), lambda i, j: (i, j, 0))
    tab_t = pl.BlockSpec((1, QK_ROPE, ts), lambda i, j: (i, 0, j))
    return pl.pallas_call(
        _rope_table_kernel,
        grid=(b, s // ts),
        in_specs=[pl.BlockSpec((1, 1, ts), lambda i, j: (i, 0, j)),
                  pl.BlockSpec((half, 1), lambda i, j: (0, 0))],
        out_specs=[tab, tab, tab_t, tab_t],
        out_shape=[jax.ShapeDtypeStruct((b, s, LANES), F32)] * 2
                  + [jax.ShapeDtypeStruct((b, QK_ROPE, s), F32)] * 2,
        compiler_params=_params(("parallel", "parallel")),
        name="rope_tables",
    )(pos, inv[:, None])


def _mods_kernel(c_ref, w_ref, b_ref, o_ref):
    c = c_ref[...]
    sc = (c * jax.nn.sigmoid(c)).astype(BF16)
    o_ref[0] = _dot(sc, w_ref[0].astype(BF16)) + b_ref[0]


def _mods(c, mod_w, mod_b):
    depth, d, n = mod_w.shape
    b = c.shape[0]
    nt = MODS_N_TILE
    return pl.pallas_call(
        _mods_kernel,
        grid=(depth, n // nt),
        in_specs=[pl.BlockSpec((b, d), lambda l, j: (0, 0)),
                  pl.BlockSpec((1, d, nt), lambda l, j: (l, 0, j)),
                  pl.BlockSpec((1, 1, nt), lambda l, j: (l, 0, j))],
        out_specs=pl.BlockSpec((1, b, nt), lambda l, j: (l, 0, j)),
        out_shape=jax.ShapeDtypeStruct((depth, b, n), F32),
        compiler_params=_params(("parallel", "parallel")),
        name="adaln_mods",
    )(c, mod_w, mod_b.reshape(depth, 1, n))


def _pool_kernel(x_ref, mod_ref, g_ref, w_ref, pb_ref, ps_ref, o_ref, hbuf):
    s = pl.program_id(1)
    ts = x_ref.shape[1]
    d = x_ref.shape[2]
    group = d // len(POOL_WINDOWS)

    @pl.when(s == 0)
    def _():
        hbuf[0:POOL_HALO, :] = jnp.zeros((POOL_HALO, d), F32)

    x = x_ref[0]
    mod = mod_ref[0]
    h = _rmsnorm(x, g_ref[...]) * (1.0 + mod[1:2]) + mod[0:1]
    hbuf[POOL_HALO:POOL_HALO + ts, :] = h
    t = s * ts + lax.broadcasted_iota(jnp.int32, (ts, 1), 0)
    ys = []
    for gi, w in enumerate(POOL_WINDOWS):
        lo = gi * group
        hg = h[:, lo:lo + group]
        acc = hg
        for k in range(1, w):
            acc = acc + hbuf[POOL_HALO - k:POOL_HALO - k + ts, lo:lo + group]
        cnt = jnp.minimum(t + 1, w).astype(F32)
        pooled = acc / cnt - hg
        ys.append(_dot(pooled.astype(BF16), w_ref[gi]))
    y = (jnp.concatenate(ys, axis=-1) + pb_ref[...]) * ps_ref[...]
    o_ref[0] = x + mod[2:3] * y
    hbuf[0:POOL_HALO, :] = hbuf[ts:ts + POOL_HALO, :]


def _pool_layer(x, mods_l, g, w, pb, ps):
    b, s, d = x.shape
    ts = SEQ_TILE
    vec = pl.BlockSpec((1, d), lambda i, j: (0, 0))
    tile = pl.BlockSpec((1, ts, d), lambda i, j: (i, j, 0))
    return pl.pallas_call(
        _pool_kernel,
        grid=(b, s // ts),
        in_specs=[tile,
                  pl.BlockSpec((1, N_MOD, d), lambda i, j: (i, 0, 0)),
                  vec,
                  pl.BlockSpec(w.shape, lambda i, j: (0, 0, 0)),
                  vec, vec],
        out_specs=tile,
        out_shape=jax.ShapeDtypeStruct(x.shape, F32),
        scratch_shapes=[pltpu.VMEM((ts + POOL_HALO, d), F32)],
        compiler_params=_params(("arbitrary", "arbitrary")),
        name="pool_mixer",
    )(x, mods_l, g, w, pb, ps)


def _shift_rows(a, k, prev_rows):
    rolled = pltpu.roll(a, k, axis=0)
    prev = pltpu.roll(prev_rows, k, axis=0)
    row = lax.broadcasted_iota(jnp.int32, prev_rows.shape, 0)
    head = jnp.where(row < k, prev, rolled[0:SUBLANES])
    return jnp.concatenate([head, rolled[SUBLANES:]], axis=0)


def _ffn_kernel(*refs, with_attn, with_final):
    it = iter(refs)
    x_ref = next(it)
    mod_ref = next(it)
    if with_attn:
        o_ref_in = next(it)
        wo_ref = next(it)
    g_ref = next(it)
    wa_ref = next(it)
    wv_ref = next(it)
    cw_ref = next(it)
    cb_ref = next(it)
    wd_ref = next(it)
    if with_final:
        fg_ref = next(it)
    out_ref = next(it)
    carry_ref = next(it)
    acc_ref = next(it)

    s = pl.program_id(1)
    ts = x_ref.shape[1]
    n_chunks = wa_ref.shape[0]

    @pl.when(s == 0)
    def _():
        carry_ref[...] = jnp.zeros(carry_ref.shape, F32)

    x = x_ref[0]
    mod = mod_ref[0]
    if with_attn:
        x = x + mod[2:3] * _dot(o_ref_in[0], wo_ref[...])
    h = (_rmsnorm(x, g_ref[...]) * (1.0 + mod[4:5]) + mod[3:4]).astype(BF16)

    sqrt_half = math.sqrt(0.5)
    for j in range(n_chunks):
        a = _dot(h, wa_ref[j])
        v = _dot(h, wv_ref[j])
        prev = carry_ref[j]
        cw = cw_ref[j]
        a1 = _shift_rows(a, 1, prev)
        a2 = _shift_rows(a, 2, prev)
        conv = a2 * cw[0:1] + a1 * cw[1:2] + a * cw[2:3] + cb_ref[j]
        gate = 0.5 * conv * (1.0 + lax.erf(conv * sqrt_half))
        contrib = _dot((gate * v).astype(BF16), wd_ref[j])
        if j == 0:
            acc_ref[...] = contrib
        else:
            acc_ref[...] += contrib
        carry_ref[j] = a[ts - SUBLANES:ts]

    y = x + mod[5:6] * acc_ref[...]
    if with_final:
        y = _rmsnorm(y, fg_ref[...])
    out_ref[0] = y


def _ffn_layer(x, mods_l, g, wa, wv, cw, cb, wd, attn=None, final_g=None):
    b, s, d = x.shape
    ts = SEQ_TILE
    n_chunks, _, fc = wa.shape
    const2 = lambda i, j: (0, 0)
    const3 = lambda i, j: (0, 0, 0)
    vec = pl.BlockSpec((1, d), const2)
    tile = pl.BlockSpec((1, ts, d), lambda i, j: (i, j, 0))
    single = dict(pipeline_mode=pl.Buffered(1))
    in_specs = [tile, pl.BlockSpec((1, N_MOD, d), lambda i, j: (i, 0, 0))]
    args = [x, mods_l]
    if attn is not None:
        o, wo = attn
        in_specs += [pl.BlockSpec((1, ts, o.shape[2]), lambda i, j: (i, j, 0)),
                     pl.BlockSpec(wo.shape, const2, **single)]
        args += [o, wo]
    in_specs += [vec,
                 pl.BlockSpec(wa.shape, const3, **single),
                 pl.BlockSpec(wv.shape, const3, **single),
                 pl.BlockSpec(cw.shape, const3),
                 pl.BlockSpec(cb.shape, const3),
                 pl.BlockSpec(wd.shape, const3, **single)]
    args += [g, wa, wv, cw, cb, wd]
    if final_g is not None:
        in_specs.append(vec)
        args.append(final_g)
    kern = functools.partial(_ffn_kernel, with_attn=attn is not None, with_final=final_g is not None)
    return pl.pallas_call(
        kern,
        grid=(b, s // ts),
        in_specs=in_specs,
        out_specs=tile,
        out_shape=jax.ShapeDtypeStruct(x.shape, F32),
        scratch_shapes=[pltpu.VMEM((n_chunks, SUBLANES, fc), F32),
                        pltpu.VMEM((ts, d), F32)],
        compiler_params=_params(("arbitrary", "arbitrary")),
        name="convglu_ffn",
    )(*args)


def _kv_kernel(x_ref, g_ref, wc_ref, wr_ref, wrs_ref, cg_ref, wuk_ref, wuvt_ref, cc_ref, ss_ref,
               k_ref, vt_ref):
    xn = _rmsnorm(x_ref[0], g_ref[...]).astype(BF16)
    c_kv = _rmsnorm(_dot(xn, wc_ref[...]), cg_ref[...]).astype(BF16)
    cc = cc_ref[0][:, 0:QK_ROPE]
    ss = ss_ref[0][:, 0:QK_ROPE]
    k_rope = (_dot(xn, wr_ref[...]) * cc + _dot(xn, wrs_ref[...]) * ss).astype(BF16)
    k_nope = _dot(c_kv, wuk_ref[...])
    vt = lax.dot_general(wuvt_ref[...], c_kv, (((1,), (1,)), ((), ())), preferred_element_type=F32)
    for hd in range(N_HEADS):
        k_ref[0, hd, :, 0:QK_NOPE] = k_nope[:, hd * QK_NOPE:(hd + 1) * QK_NOPE].astype(BF16)
        k_ref[0, hd, :, QK_NOPE:QK_HEAD] = k_rope
        vt_ref[0, hd] = vt[hd * V_HEAD:(hd + 1) * V_HEAD, :].astype(BF16)


def _shared_kv(x, g, wc, wr, wrs, cg, wuk, wuvt, cc, ss):
    b, s, d = x.shape
    ts = SEQ_TILE
    const2 = lambda i, j: (0, 0)
    full = lambda a: pl.BlockSpec(a.shape, const2)
    tab = pl.BlockSpec((1, ts, LANES), lambda i, j: (i, j, 0))
    return pl.pallas_call(
        _kv_kernel,
        grid=(b, s // ts),
        in_specs=[pl.BlockSpec((1, ts, d), lambda i, j: (i, j, 0)),
                  full(g), full(wc), full(wr), full(wrs), full(cg), full(wuk), full(wuvt), tab, tab],
        out_specs=[pl.BlockSpec((1, N_HEADS, ts, QK_HEAD), lambda i, j: (i, 0, j, 0)),
                   pl.BlockSpec((1, N_HEADS, V_HEAD, ts), lambda i, j: (i, 0, 0, j))],
        out_shape=[jax.ShapeDtypeStruct((b, N_HEADS, s, QK_HEAD), BF16),
                   jax.ShapeDtypeStruct((b, N_HEADS, V_HEAD, s), BF16)],
        compiler_params=_params(("parallel", "parallel")),
        name="shared_kv",
    )(x, g, wc, wr, wrs, cg, wuk, wuvt, cc, ss)


def _q_kernel(x_ref, mod_ref, g_ref, wdq_ref, qg_ref, wuqt_ref, cct_ref, sst_ref, qt_ref):
    mod = mod_ref[0]
    h = (_rmsnorm(x_ref[0], g_ref[...]) * (1.0 + mod[1:2]) + mod[0:1]).astype(BF16)
    c_q = _rmsnorm(_dot(h, wdq_ref[...]), qg_ref[...]).astype(BF16)
    scale = QK_HEAD ** -0.5 * math.log2(math.e)
    n_nope = N_HEADS * QK_NOPE
    n_rope = N_HEADS * QK_ROPE
    qt = lax.dot_general(wuqt_ref[...], c_q, (((1,), (1,)), ((), ())), preferred_element_type=F32)
    cct = jnp.concatenate([cct_ref[0]] * N_HEADS, axis=0)
    sst = jnp.concatenate([sst_ref[0]] * N_HEADS, axis=0)
    q_nope = qt[0:n_nope] * scale
    q_rope = (qt[n_nope:n_nope + n_rope] * cct + qt[n_nope + n_rope:n_nope + 2 * n_rope] * sst) * scale
    for hd in range(N_HEADS):
        qt_ref[0, hd, 0:QK_NOPE, :] = q_nope[hd * QK_NOPE:(hd + 1) * QK_NOPE].astype(BF16)
        qt_ref[0, hd, QK_NOPE:QK_HEAD, :] = q_rope[hd * QK_ROPE:(hd + 1) * QK_ROPE].astype(BF16)


def _q_proj(x, mods_l, g, wdq, qg, wuqt, cct, sst):
    b, s, d = x.shape
    ts = SEQ_TILE
    const2 = lambda i, j: (0, 0)
    full = lambda a: pl.BlockSpec(a.shape, const2)
    tab_t = pl.BlockSpec((1, QK_ROPE, ts), lambda i, j: (i, 0, j))
    return pl.pallas_call(
        _q_kernel,
        grid=(b, s // ts),
        in_specs=[pl.BlockSpec((1, ts, d), lambda i, j: (i, j, 0)),
                  pl.BlockSpec((1, N_MOD, d), lambda i, j: (i, 0, 0)),
                  full(g), full(wdq), full(qg), full(wuqt), tab_t, tab_t],
        out_specs=pl.BlockSpec((1, N_HEADS, QK_HEAD, ts), lambda i, j: (i, 0, 0, j)),
        out_shape=jax.ShapeDtypeStruct((b, N_HEADS, QK_HEAD, s), BF16),
        compiler_params=_params(("parallel", "parallel")),
        name="q_proj",
    )(x, mods_l, g, wdq, qg, wuqt, cct, sst)


def _attn_kernel(qt_ref, k_ref, vt_ref, o_ref, acc_ref, s0_ref, s1_ref, s2_ref, s3_ref,
                 p0_ref, p1_ref, p2_ref, p3_ref):
    qi = pl.program_id(2)
    tq = qt_ref.shape[3]
    tk = s0_ref.shape[0]
    qt = qt_ref[0, 0]
    acc_ref[...] = jnp.zeros(acc_ref.shape, F32)
    p2_ref[...] = jnp.zeros(p2_ref.shape, BF16)
    p3_ref[...] = jnp.zeros(p3_ref.shape, BF16)

    def scores(c, s_ref, lo=0):
        start = pl.multiple_of(c * tk, tk)
        st = _dot(k_ref[0, 0, pl.ds(start, tk), :], qt[:, lo:])
        s_ref[:, lo:] = st
        return jnp.max(st, axis=0, keepdims=True)

    def softmax_step(st, st_max, p_ref, m_prev, l_prev, lo=0):
        m_new = jnp.maximum(m_prev[:, lo:], st_max)
        alpha = jnp.exp2(m_prev[:, lo:] - m_new)
        p = jnp.exp2(st - m_new)
        p_ref[:, lo:] = p.astype(BF16)
        l_new = alpha * l_prev[:, lo:] + jnp.sum(p, axis=0, keepdims=True)
        if lo:
            m_new = jnp.concatenate([m_prev[:, :lo], m_new], axis=1)
            l_new = jnp.concatenate([l_prev[:, :lo], l_new], axis=1)
        return alpha, m_new, l_new

    def accumulate(c, p_ref, alpha, lo=0):
        start = pl.multiple_of(jnp.maximum(c, 0) * tk, tk)
        vt = vt_ref[0, 0, :, pl.ds(start, tk)]
        acc_ref[:, lo:] = alpha * acc_ref[:, lo:] + _dot(vt, p_ref[:, lo:])

    def diagonal(s_ref, lo):
        st = s_ref[:, lo:]
        key = lax.broadcasted_iota(jnp.int32, st.shape, 0)
        qry = lax.broadcasted_iota(jnp.int32, st.shape, 1)
        st = jnp.where(key <= qry, st, jnp.finfo(F32).min)
        return st, jnp.max(st, axis=0, keepdims=True)

    def half_trip(c, r, w, maxes, alphas, m_run, l_run):
        new_alphas = []
        for i in range(2):
            alpha, m_run, l_run = softmax_step(s_refs[r + i][...], maxes[i], p_refs[r + i], m_run, l_run)
            new_alphas.append(alpha)
        new_maxes = [scores(c + 2 + i, s_refs[w + i]) for i in range(2)]
        for i in range(2):
            accumulate(c - 2 + i, p_refs[w + i], alphas[i])
        return new_maxes, new_alphas, m_run, l_run

    def body(t, carry):
        mx0, mx1, al0, al1, m_run, l_run = carry
        c0 = 4 * t
        maxes, alphas, m_run, l_run = half_trip(c0, 0, 2, (mx0, mx1), (al0, al1), m_run, l_run)
        maxes, alphas, m_run, l_run = half_trip(c0 + 2, 2, 0, maxes, alphas, m_run, l_run)
        return maxes[0], maxes[1], alphas[0], alphas[1], m_run, l_run

    n_diag = tq // tk
    s_refs = (s0_ref, s1_ref, s2_ref, s3_ref)
    p_refs = (p0_ref, p1_ref, p2_ref, p3_ref)
    stat = lambda val: jnp.full((1, tq), val, F32)
    init = (scores(0, s0_ref), scores(1, s1_ref), stat(1.0), stat(1.0), stat(-jnp.inf), stat(0.0))
    _, _, al0, al1, m_run, l_run = lax.fori_loop(0, qi, body, init)

    d0 = n_diag * qi
    scores(d0 + 2, s2_ref, 2 * tk)
    scores(d0 + 3, s3_ref, 3 * tk)
    accumulate(d0 - 2, p2_ref, al0)
    accumulate(d0 - 1, p3_ref, al1)
    alphas = []
    for d in range(n_diag):
        if d == 2:
            accumulate(d0, p0_ref, alphas[0], 0)
            accumulate(d0 + 1, p1_ref, alphas[1], tk)
        st, st_max = diagonal(s_refs[d], d * tk)
        alpha, m_run, l_run = softmax_step(st, st_max, p_refs[d], m_run, l_run, d * tk)
        alphas.append(alpha)
    accumulate(d0 + 2, p2_ref, alphas[2], 2 * tk)
    accumulate(d0 + 3, p3_ref, alphas[3], 3 * tk)
    o_ref[0] = (acc_ref[...] / l_run).T.astype(o_ref.dtype)


def _attention(qt, k, vt):
    b, nh, dk, s = qt.shape
    dv = vt.shape[2]
    tq = ATTN_TILE
    tk = ATTN_KEY_CHUNK
    assert tq == 4 * tk and s % tq == 0
    return pl.pallas_call(
        _attn_kernel,
        grid=(b, nh, s // tq),
        in_specs=[pl.BlockSpec((1, 1, dk, tq), lambda i, h, j: (i, h, 0, j)),
                  pl.BlockSpec((1, 1, s, dk), lambda i, h, j: (i, h, 0, 0)),
                  pl.BlockSpec((1, 1, dv, s), lambda i, h, j: (i, h, 0, 0))],
        out_specs=pl.BlockSpec((1, tq, dv), lambda i, h, j: (i, j, h)),
        out_shape=jax.ShapeDtypeStruct((b, s, nh * dv), BF16),
        scratch_shapes=[pltpu.VMEM((dv, tq), F32),
                        *[pltpu.VMEM((tk, tq), F32)] * 4, *[pltpu.VMEM((tk, tq), BF16)] * 4],
        compiler_params=_params(("parallel", "parallel", "arbitrary")),
        name="causal_attention",
    )(qt, k, vt)


def _swap_rope_halves(w):
    half = QK_ROPE // 2
    return jnp.concatenate([w[..., half:], w[..., :half]], axis=-1)


def _prep_ffn(w_up, conv_w, conv_b, w_down):
    d, two_f = w_up.shape
    f = two_f // 2
    n_chunks = f // FF_CHUNK
    wu = w_up.astype(BF16)
    wa = wu[:, :f].reshape(d, n_chunks, FF_CHUNK).transpose(1, 0, 2)
    wv = wu[:, f:].reshape(d, n_chunks, FF_CHUNK).transpose(1, 0, 2)
    cw = conv_w.reshape(CONV_WIDTH, n_chunks, FF_CHUNK).transpose(1, 0, 2)
    cb = conv_b.reshape(n_chunks, 1, FF_CHUNK)
    wd = w_down.astype(BF16).reshape(n_chunks, FF_CHUNK, d)
    return wa, wv, cw, cb, wd


def _prep_wuq(w_uq):
    r = w_uq.shape[0]
    w = w_uq.reshape(r, N_HEADS, QK_HEAD)
    nope = w[:, :, :QK_NOPE].reshape(r, N_HEADS * QK_NOPE)
    rope = w[:, :, QK_NOPE:]
    rope_n = rope.reshape(r, N_HEADS * QK_ROPE)
    rope_s = _swap_rope_halves(rope).reshape(r, N_HEADS * QK_ROPE)
    return jnp.concatenate([nope, rope_n, rope_s], axis=-1).T.astype(BF16)


def kernel(x, c, positions, mod_w, mod_b, norm1_g, norm2_g, pool_w, pool_b, pool_scale, kv_in_g,
           w_dkv, ckv_norm_g, w_uk, w_uv, w_dq, q_norm_g, w_uq, w_o, w_up, conv_w, conv_b, w_down,
           final_g):
    b, s, d = x.shape
    depth = mod_w.shape[0]
    n_pool = pool_w.shape[0]
    kv_rank = ckv_norm_g.shape[0]

    cc, ss, cct, sst = _rope_tables(positions)
    mods = _mods(c, mod_w, mod_b).reshape(depth, b, N_MOD, d)
    row = lambda a: a.reshape(1, -1)

    k = vt = None
    for l in range(depth):
        ffn_w = _prep_ffn(w_up[l], conv_w[l], conv_b[l], w_down[l])
        final = row(final_g) if l == depth - 1 else None
        if l < n_pool:
            x = _pool_layer(x, mods[l], row(norm1_g[l]), pool_w[l].astype(BF16), row(pool_b[l]),
                            row(pool_scale[l]))
            x = _ffn_layer(x, mods[l], row(norm2_g[l]), *ffn_w, final_g=final)
        else:
            j = l - n_pool
            q = _q_proj(x, mods[l], row(norm1_g[l]), w_dq[j].astype(BF16), row(q_norm_g[j]),
                        _prep_wuq(w_uq[j]), cct, sst)
            o = _attention(q, k, vt)
            x = _ffn_layer(x, mods[l], row(norm2_g[l]), *ffn_w, attn=(o, w_o[j].astype(BF16)),
                           final_g=final)
        if l == n_pool - 1:
            w_rope = w_dkv[:, kv_rank:]
            k, vt = _shared_kv(x, row(kv_in_g), w_dkv[:, :kv_rank].astype(BF16), w_rope.astype(BF16),
                               _swap_rope_halves(w_rope).astype(BF16), row(ckv_norm_g),
                               w_uk.astype(BF16), w_uv.T.astype(BF16), cc, ss)
    return x
```

```python
import functools
import math

import jax
import jax.numpy as jnp
from jax import lax
from jax.experimental import pallas as pl
from jax.experimental.pallas import tpu as pltpu

F32 = jnp.float32
BF16 = jnp.bfloat16

POOL_WINDOWS = (2, 4, 8, 16)
N_HEADS = 8
QK_NOPE = 128
QK_ROPE = 64
V_HEAD = 128
QK_HEAD = QK_NOPE + QK_ROPE
ROPE_THETA = 10000.0
CONV_WIDTH = 3
EPS = 1e-6
N_MOD = 6

LANES = 128
SUBLANES = 8
VMEM_LIMIT_BYTES = 56 * 1024 * 1024

SEQ_TILE = 512
FF_CHUNK = 256
ATTN_TILE = 1024
ATTN_KEY_CHUNK = 256
POOL_HALO = 16
MODS_N_TILE = 1536


def _params(sem, flags=None):
    return pltpu.CompilerParams(dimension_semantics=sem, vmem_limit_bytes=VMEM_LIMIT_BYTES,
                                flags=flags)


def _rmsnorm(x, g):
    ms = jnp.mean(x * x, axis=-1, keepdims=True)
    return x * lax.rsqrt(ms + EPS) * g


def _dot(a, b):
    return jnp.dot(a, b, preferred_element_type=F32)


def _rope_table_kernel(pos_ref, inv_ref, cc_ref, ss_ref, cct_ref, sst_ref):
    ang = inv_ref[...] * pos_ref[0]
    cos_t = jnp.cos(ang)
    sin_t = jnp.sin(ang)
    cct = jnp.concatenate([cos_t, cos_t], axis=0)
    sst = jnp.concatenate([-sin_t, sin_t], axis=0)
    cct_ref[0] = cct
    sst_ref[0] = sst
    cc_ref[0] = jnp.concatenate([cct, cct], axis=0).T
    ss_ref[0] = jnp.concatenate([sst, sst], axis=0).T


def _rope_tables(positions):
    b, s = positions.shape
    half = QK_ROPE // 2
    inv = 1.0 / (ROPE_THETA ** (jnp.arange(0, QK_ROPE, 2, dtype=F32) / QK_ROPE))
    pos = positions.astype(F32)[:, None, :]
    ts = SEQ_TILE
    tab = pl.BlockSpec((1, ts, LANES), lambda i, j: (i, j, 0))
    tab_t = pl.BlockSpec((1, QK_ROPE, ts), lambda i, j: (i, 0, j))
    return pl.pallas_call(
        _rope_table_kernel,
        grid=(b, s // ts),
        in_specs=[pl.BlockSpec((1, 1, ts), lambda i, j: (i, 0, j)),
                  pl.BlockSpec((half, 1), lambda i, j: (0, 0))],
        out_specs=[tab, tab, tab_t, tab_t],
        out_shape=[jax.ShapeDtypeStruct((b, s, LANES), F32)] * 2
                  + [jax.ShapeDtypeStruct((b, QK_ROPE, s), F32)] * 2,
        compiler_params=_params(("parallel", "parallel")),
        name="rope_tables",
    )(pos, inv[:, None])


def _mods_kernel(c_ref, w_ref, b_ref, o_ref):
    c = c_ref[...]
    sc = (c * jax.nn.sigmoid(c)).astype(BF16)
    o_ref[0] = _dot(sc, w_ref[0].astype(BF16)) + b_ref[0]


def _mods(c, mod_w, mod_b):
    depth, d, n = mod_w.shape
    b = c.shape[0]
    nt = MODS_N_TILE
    return pl.pallas_call(
        _mods_kernel,
        grid=(depth, n // nt),
        in_specs=[pl.BlockSpec((b, d), lambda l, j: (0, 0)),
                  pl.BlockSpec((1, d, nt), lambda l, j: (l, 0, j)),
                  pl.BlockSpec((1, 1, nt), lambda l, j: (l, 0, j))],
        out_specs=pl.BlockSpec((1, b, nt), lambda l, j: (l, 0, j)),
        out_shape=jax.ShapeDtypeStruct((depth, b, n), F32),
        compiler_params=_params(("parallel", "parallel")),
        name="adaln_mods",
    )(c, mod_w, mod_b.reshape(depth, 1, n))


def _pool_kernel(x_ref, mod_ref, g_ref, w_ref, pb_ref, ps_ref, o_ref, halo_ref):
    s = pl.program_id(1)
    ts = x_ref.shape[1]
    d = x_ref.shape[2]
    group = d // len(POOL_WINDOWS)

    @pl.when(s == 0)
    def _():
        halo_ref[...] = jnp.zeros(halo_ref.shape, F32)

    x = x_ref[0]
    mod = mod_ref[0]
    h = _rmsnorm(x, g_ref[...]) * (1.0 + mod[1:2]) + mod[0:1]
    t = s * ts + lax.broadcasted_iota(jnp.int32, (ts, 1), 0)
    ys = []
    for gi, w in enumerate(POOL_WINDOWS):
        lo = gi * group
        hg = h[:, lo:lo + group]
        win = jnp.concatenate([halo_ref[:, lo:lo + group], hg], axis=0)
        k = 1
        while k < w:
            win = win + pltpu.roll(win, k, axis=0)
            k *= 2
        cnt = jnp.minimum(t + 1, w).astype(F32)
        pooled = win[POOL_HALO:] / cnt - hg
        ys.append(_dot(pooled.astype(BF16), w_ref[gi]))
    y = (jnp.concatenate(ys, axis=-1) + pb_ref[...]) * ps_ref[...]
    o_ref[0] = x + mod[2:3] * y
    halo_ref[...] = h[ts - POOL_HALO:ts]


def _pool_layer(x, mods_l, g, w, pb, ps):
    b, s, d = x.shape
    ts = SEQ_TILE
    vec = pl.BlockSpec((1, d), lambda i, j: (0, 0))
    tile = pl.BlockSpec((1, ts, d), lambda i, j: (i, j, 0))
    return pl.pallas_call(
        _pool_kernel,
        grid=(b, s // ts),
        in_specs=[tile,
                  pl.BlockSpec((1, N_MOD, d), lambda i, j: (i, 0, 0)),
                  vec,
                  pl.BlockSpec(w.shape, lambda i, j: (0, 0, 0)),
                  vec, vec],
        out_specs=tile,
        out_shape=jax.ShapeDtypeStruct(x.shape, F32),
        scratch_shapes=[pltpu.VMEM((POOL_HALO, d), F32)],
        compiler_params=_params(("arbitrary", "arbitrary")),
        name="pool_mixer",
    )(x, mods_l, g, w, pb, ps)


def _shift_rows(a, k, prev_rows):
    rolled = pltpu.roll(a, k, axis=0)
    prev = pltpu.roll(prev_rows, k, axis=0)
    row = lax.broadcasted_iota(jnp.int32, prev_rows.shape, 0)
    head = jnp.where(row < k, prev, rolled[0:SUBLANES])
    return jnp.concatenate([head, rolled[SUBLANES:]], axis=0)


def _ffn_kernel(*refs, with_attn, with_final):
    it = iter(refs)
    x_ref = next(it)
    mod_ref = next(it)
    if with_attn:
        o_ref_in = next(it)
        wo_ref = next(it)
    g_ref = next(it)
    wup_ref = next(it)
    cw_ref = next(it)
    cb_ref = next(it)
    wd_ref = next(it)
    if with_final:
        fg_ref = next(it)
    out_ref = next(it)
    carry_ref = next(it)
    gate_ref = next(it)

    s = pl.program_id(1)
    ts = x_ref.shape[1]
    f = wd_ref.shape[0]
    fc = FF_CHUNK
    n_chunks = f // fc

    @pl.when(s == 0)
    def _():
        carry_ref[...] = jnp.zeros(carry_ref.shape, F32)

    x = x_ref[0]
    mod = mod_ref[0]
    if with_attn:
        x = x + mod[2:3] * _dot(o_ref_in[0], wo_ref[...])
    h = (_rmsnorm(x, g_ref[...]) * (1.0 + mod[4:5]) + mod[3:4]).astype(BF16)

    sqrt_half = math.sqrt(0.5)
    up = lambda j: (_dot(h, wup_ref[:, j * fc:(j + 1) * fc]),
                    _dot(h, wup_ref[:, f + j * fc:f + (j + 1) * fc]))
    nxt = up(0)
    for j in range(n_chunks):
        cols = slice(j * fc, (j + 1) * fc)
        a, v = nxt
        if j + 1 < n_chunks:
            nxt = up(j + 1)
        prev = carry_ref[:, cols]
        a1 = _shift_rows(a, 1, prev)
        a2 = _shift_rows(a, 2, prev)
        conv = a2 * cw_ref[0:1, cols] + a1 * cw_ref[1:2, cols] + a * cw_ref[2:3, cols] + cb_ref[:, cols]
        gate = 0.5 * conv * (1.0 + lax.erf(conv * sqrt_half))
        gate_ref[:, cols] = (gate * v).astype(BF16)
        carry_ref[:, cols] = a[ts - SUBLANES:ts]

    y = x + mod[5:6] * _dot(gate_ref[...], wd_ref[...])
    if with_final:
        y = _rmsnorm(y, fg_ref[...])
    out_ref[0] = y


def _ffn_layer(x, mods_l, g, wup, cw, cb, wd, attn=None, final_g=None):
    b, s, d = x.shape
    ts = SEQ_TILE
    f = wd.shape[0]
    assert f % FF_CHUNK == 0
    const2 = lambda i, j: (0, 0)
    vec = pl.BlockSpec((1, d), const2)
    tile = pl.BlockSpec((1, ts, d), lambda i, j: (i, j, 0))
    single = dict(pipeline_mode=pl.Buffered(1))
    in_specs = [tile, pl.BlockSpec((1, N_MOD, d), lambda i, j: (i, 0, 0))]
    args = [x, mods_l]
    if attn is not None:
        o, wo = attn
        in_specs += [pl.BlockSpec((1, ts, o.shape[2]), lambda i, j: (i, j, 0)),
                     pl.BlockSpec(wo.shape, const2, **single)]
        args += [o, wo]
    in_specs += [vec,
                 pl.BlockSpec(wup.shape, const2, **single),
                 pl.BlockSpec(cw.shape, const2),
                 pl.BlockSpec(cb.shape, const2),
                 pl.BlockSpec(wd.shape, const2, **single)]
    args += [g, wup, cw, cb, wd]
    if final_g is not None:
        in_specs.append(vec)
        args.append(final_g)
    kern = functools.partial(_ffn_kernel, with_attn=attn is not None, with_final=final_g is not None)
    return pl.pallas_call(
        kern,
        grid=(b, s // ts),
        in_specs=in_specs,
        out_specs=tile,
        out_shape=jax.ShapeDtypeStruct(x.shape, F32),
        scratch_shapes=[pltpu.VMEM((SUBLANES, f), F32),
                        pltpu.VMEM((ts, f), BF16)],
        compiler_params=_params(("arbitrary", "arbitrary")),
        name="convglu_ffn",
    )(*args)


def _kv_kernel(x_ref, g_ref, wc_ref, wr_ref, wrs_ref, cg_ref, wuk_ref, wuvt_ref, cc_ref, ss_ref,
               k_ref, vt_ref):
    xn = _rmsnorm(x_ref[0], g_ref[...]).astype(BF16)
    c_kv = _rmsnorm(_dot(xn, wc_ref[...]), cg_ref[...]).astype(BF16)
    cc = cc_ref[0][:, 0:QK_ROPE]
    ss = ss_ref[0][:, 0:QK_ROPE]
    k_rope = (_dot(xn, wr_ref[...]) * cc + _dot(xn, wrs_ref[...]) * ss).astype(BF16)
    k_nope = _dot(c_kv, wuk_ref[...])
    vt = lax.dot_general(wuvt_ref[...], c_kv, (((1,), (1,)), ((), ())), preferred_element_type=F32)
    for hd in range(N_HEADS):
        k_ref[0, hd, :, 0:QK_NOPE] = k_nope[:, hd * QK_NOPE:(hd + 1) * QK_NOPE].astype(BF16)
        k_ref[0, hd, :, QK_NOPE:QK_HEAD] = k_rope
        vt_ref[0, hd] = vt[hd * V_HEAD:(hd + 1) * V_HEAD, :].astype(BF16)


def _shared_kv(x, g, wc, wr, wrs, cg, wuk, wuvt, cc, ss):
    b, s, d = x.shape
    ts = SEQ_TILE
    const2 = lambda i, j: (0, 0)
    full = lambda a: pl.BlockSpec(a.shape, const2)
    tab = pl.BlockSpec((1, ts, LANES), lambda i, j: (i, j, 0))
    return pl.pallas_call(
        _kv_kernel,
        grid=(b, s // ts),
        in_specs=[pl.BlockSpec((1, ts, d), lambda i, j: (i, j, 0)),
                  full(g), full(wc), full(wr), full(wrs), full(cg), full(wuk), full(wuvt), tab, tab],
        out_specs=[pl.BlockSpec((1, N_HEADS, ts, QK_HEAD), lambda i, j: (i, 0, j, 0)),
                   pl.BlockSpec((1, N_HEADS, V_HEAD, ts), lambda i, j: (i, 0, 0, j))],
        out_shape=[jax.ShapeDtypeStruct((b, N_HEADS, s, QK_HEAD), BF16),
                   jax.ShapeDtypeStruct((b, N_HEADS, V_HEAD, s), BF16)],
        compiler_params=_params(("parallel", "parallel")),
        name="shared_kv",
    )(x, g, wc, wr, wrs, cg, wuk, wuvt, cc, ss)


def _q_kernel(x_ref, mod_ref, g_ref, wdq_ref, qg_ref, wuqt_ref, cct_ref, sst_ref, qt_ref):
    mod = mod_ref[0]
    h = (_rmsnorm(x_ref[0], g_ref[...]) * (1.0 + mod[1:2]) + mod[0:1]).astype(BF16)
    c_q = _rmsnorm(_dot(h, wdq_ref[...]), qg_ref[...]).astype(BF16)
    scale = QK_HEAD ** -0.5 * math.log2(math.e)
    n_nope = N_HEADS * QK_NOPE
    n_rope = N_HEADS * QK_ROPE
    qt = lax.dot_general(wuqt_ref[...], c_q, (((1,), (1,)), ((), ())), preferred_element_type=F32)
    cct = jnp.concatenate([cct_ref[0]] * N_HEADS, axis=0)
    sst = jnp.concatenate([sst_ref[0]] * N_HEADS, axis=0)
    q_nope = qt[0:n_nope] * scale
    q_rope = (qt[n_nope:n_nope + n_rope] * cct + qt[n_nope + n_rope:n_nope + 2 * n_rope] * sst) * scale
    for hd in range(N_HEADS):
        qt_ref[0, hd, 0:QK_NOPE, :] = q_nope[hd * QK_NOPE:(hd + 1) * QK_NOPE].astype(BF16)
        qt_ref[0, hd, QK_NOPE:QK_HEAD, :] = q_rope[hd * QK_ROPE:(hd + 1) * QK_ROPE].astype(BF16)


def _q_proj(x, mods_l, g, wdq, qg, wuqt, cct, sst):
    b, s, d = x.shape
    ts = SEQ_TILE
    const2 = lambda i, j: (0, 0)
    full = lambda a: pl.BlockSpec(a.shape, const2)
    tab_t = pl.BlockSpec((1, QK_ROPE, ts), lambda i, j: (i, 0, j))
    return pl.pallas_call(
        _q_kernel,
        grid=(b, s // ts),
        in_specs=[pl.BlockSpec((1, ts, d), lambda i, j: (i, j, 0)),
                  pl.BlockSpec((1, N_MOD, d), lambda i, j: (i, 0, 0)),
                  full(g), full(wdq), full(qg), full(wuqt), tab_t, tab_t],
        out_specs=pl.BlockSpec((1, N_HEADS, QK_HEAD, ts), lambda i, j: (i, 0, 0, j)),
        out_shape=jax.ShapeDtypeStruct((b, N_HEADS, QK_HEAD, s), BF16),
        compiler_params=_params(("parallel", "parallel")),
        name="q_proj",
    )(x, mods_l, g, wdq, qg, wuqt, cct, sst)


def _attn_kernel(qt_ref, k_ref, vt_ref, o_ref, acc_ref, s0_ref, s1_ref, s2_ref, s3_ref,
                 p0_ref, p1_ref, p2_ref, p3_ref):
    qi = pl.program_id(2)
    tq = qt_ref.shape[3]
    tk = s0_ref.shape[0]
    qt = qt_ref[0, 0]
    acc_ref[...] = jnp.zeros(acc_ref.shape, F32)
    p2_ref[...] = jnp.zeros(p2_ref.shape, BF16)
    p3_ref[...] = jnp.zeros(p3_ref.shape, BF16)

    def scores(c, s_ref, lo=0):
        start = pl.multiple_of(c * tk, tk)
        st = _dot(k_ref[0, 0, pl.ds(start, tk), :], qt[:, lo:])
        s_ref[:, lo:] = st
        return jnp.max(st, axis=0, keepdims=True)

    def softmax_step(st, st_max, p_ref, m_prev, l_prev, lo=0):
        m_new = jnp.maximum(m_prev[:, lo:], st_max)
        alpha = jnp.exp2(m_prev[:, lo:] - m_new)
        p = jnp.exp2(st - m_new)
        p_ref[:, lo:] = p.astype(BF16)
        l_new = alpha * l_prev[:, lo:] + jnp.sum(p, axis=0, keepdims=True)
        if lo:
            m_new = jnp.concatenate([m_prev[:, :lo], m_new], axis=1)
            l_new = jnp.concatenate([l_prev[:, :lo], l_new], axis=1)
        return alpha, m_new, l_new

    def accumulate(c, p_ref, alpha, lo=0):
        start = pl.multiple_of(jnp.maximum(c, 0) * tk, tk)
        vt = vt_ref[0, 0, :, pl.ds(start, tk)]
        acc_ref[:, lo:] = alpha * acc_ref[:, lo:] + _dot(vt, p_ref[:, lo:])

    def diagonal(s_ref, lo):
        st = s_ref[:, lo:]
        key = lax.broadcasted_iota(jnp.int32, st.shape, 0)
        qry = lax.broadcasted_iota(jnp.int32, st.shape, 1)
        st = jnp.where(key <= qry, st, jnp.finfo(F32).min)
        return st, jnp.max(st, axis=0, keepdims=True)

    def half_trip(c, r, w, maxes, alphas, m_run, l_run):
        new_alphas = []
        for i in range(2):
            alpha, m_run, l_run = softmax_step(s_refs[r + i][...], maxes[i], p_refs[r + i], m_run, l_run)
            new_alphas.append(alpha)
        new_maxes = [scores(c + 2 + i, s_refs[w + i]) for i in range(2)]
        for i in range(2):
            accumulate(c - 2 + i, p_refs[w + i], alphas[i])
        return new_maxes, new_alphas, m_run, l_run

    def body(t, carry):
        mx0, mx1, al0, al1, m_run, l_run = carry
        c0 = 4 * t
        maxes, alphas, m_run, l_run = half_trip(c0, 0, 2, (mx0, mx1), (al0, al1), m_run, l_run)
        maxes, alphas, m_run, l_run = half_trip(c0 + 2, 2, 0, maxes, alphas, m_run, l_run)
        return maxes[0], maxes[1], alphas[0], alphas[1], m_run, l_run

    n_diag = tq // tk
    s_refs = (s0_ref, s1_ref, s2_ref, s3_ref)
    p_refs = (p0_ref, p1_ref, p2_ref, p3_ref)
    stat = lambda val: jnp.full((1, tq), val, F32)
    init = (scores(0, s0_ref), scores(1, s1_ref), stat(1.0), stat(1.0), stat(-jnp.inf), stat(0.0))
    _, _, al0, al1, m_run, l_run = lax.fori_loop(0, qi, body, init)

    d0 = n_diag * qi
    scores(d0 + 2, s2_ref, 2 * tk)
    scores(d0 + 3, s3_ref, 3 * tk)
    accumulate(d0 - 2, p2_ref, al0)
    accumulate(d0 - 1, p3_ref, al1)
    alphas = []
    for d in range(n_diag):
        if d == 2:
            accumulate(d0, p0_ref, alphas[0], 0)
            accumulate(d0 + 1, p1_ref, alphas[1], tk)
        st, st_max = diagonal(s_refs[d], d * tk)
        alpha, m_run, l_run = softmax_step(st, st_max, p_refs[d], m_run, l_run, d * tk)
        alphas.append(alpha)
    accumulate(d0 + 2, p2_ref, alphas[2], 2 * tk)
    accumulate(d0 + 3, p3_ref, alphas[3], 3 * tk)
    o_ref[0] = (acc_ref[...] / l_run).T.astype(o_ref.dtype)


def _attention(qt, k, vt):
    b, nh, dk, s = qt.shape
    dv = vt.shape[2]
    tq = ATTN_TILE
    tk = ATTN_KEY_CHUNK
    assert tq == 4 * tk and s % tq == 0
    return pl.pallas_call(
        _attn_kernel,
        grid=(b, nh, s // tq),
        in_specs=[pl.BlockSpec((1, 1, dk, tq), lambda i, h, j: (i, h, 0, j)),
                  pl.BlockSpec((1, 1, s, dk), lambda i, h, j: (i, h, 0, 0)),
                  pl.BlockSpec((1, 1, dv, s), lambda i, h, j: (i, h, 0, 0))],
        out_specs=pl.BlockSpec((1, tq, dv), lambda i, h, j: (i, j, h)),
        out_shape=jax.ShapeDtypeStruct((b, s, nh * dv), BF16),
        scratch_shapes=[pltpu.VMEM((dv, tq), F32),
                        *[pltpu.VMEM((tk, tq), F32)] * 4, *[pltpu.VMEM((tk, tq), BF16)] * 4],
        compiler_params=_params(("parallel", "parallel", "arbitrary")),
        name="causal_attention",
    )(qt, k, vt)


def _swap_rope_halves(w):
    half = QK_ROPE // 2
    return jnp.concatenate([w[..., half:], w[..., :half]], axis=-1)


def _prep_wuq(w_uq):
    r = w_uq.shape[0]
    w = w_uq.reshape(r, N_HEADS, QK_HEAD)
    nope = w[:, :, :QK_NOPE].reshape(r, N_HEADS * QK_NOPE)
    rope = w[:, :, QK_NOPE:]
    rope_n = rope.reshape(r, N_HEADS * QK_ROPE)
    rope_s = _swap_rope_halves(rope).reshape(r, N_HEADS * QK_ROPE)
    return jnp.concatenate([nope, rope_n, rope_s], axis=-1).T.astype(BF16)


def kernel(x, c, positions, mod_w, mod_b, norm1_g, norm2_g, pool_w, pool_b, pool_scale, kv_in_g,
           w_dkv, ckv_norm_g, w_uk, w_uv, w_dq, q_norm_g, w_uq, w_o, w_up, conv_w, conv_b, w_down,
           final_g):
    b, s, d = x.shape
    depth = mod_w.shape[0]
    n_pool = pool_w.shape[0]
    kv_rank = ckv_norm_g.shape[0]

    cc, ss, cct, sst = _rope_tables(positions)
    mods = _mods(c, mod_w, mod_b).reshape(depth, b, N_MOD, d)
    row = lambda a: a.reshape(1, -1)

    k = vt = None
    for l in range(depth):
        ffn_w = (w_up[l].astype(BF16), conv_w[l], row(conv_b[l]), w_down[l].astype(BF16))
        final = row(final_g) if l == depth - 1 else None
        if l < n_pool:
            x = _pool_layer(x, mods[l], row(norm1_g[l]), pool_w[l].astype(BF16), row(pool_b[l]),
                            row(pool_scale[l]))
            x = _ffn_layer(x, mods[l], row(norm2_g[l]), *ffn_w, final_g=final)
        else:
            j = l - n_pool
            q = _q_proj(x, mods[l], row(norm1_g[l]), w_dq[j].astype(BF16), row(q_norm_g[j]),
                        _prep_wuq(w_uq[j]), cct, sst)
            o = _attention(q, k, vt)
            x = _ffn_layer(x, mods[l], row(norm2_g[l]), *ffn_w, attn=(o, w_o[j].astype(BF16)),
                           final_g=final)
        if l == n_pool - 1:
            w_rope = w_dkv[:, kv_rank:]
            k, vt = _shared_kv(x, row(kv_in_g), w_dkv[:, :kv_rank].astype(BF16), w_rope.astype(BF16),
                               _swap_rope_halves(w_rope).astype(BF16), row(ckv_norm_g),
                               w_uk.astype(BF16), w_uv.T.astype(BF16), cc, ss)
    return x
```

```python
import functools
import math

import jax
import jax.numpy as jnp
from jax import lax
from jax.experimental import pallas as pl
from jax.experimental.pallas import tpu as pltpu

F32 = jnp.float32
BF16 = jnp.bfloat16

POOL_WINDOWS = (2, 4, 8, 16)
N_HEADS = 8
QK_NOPE = 128
QK_ROPE = 64
V_HEAD = 128
QK_HEAD = QK_NOPE + QK_ROPE
ROPE_THETA = 10000.0
CONV_WIDTH = 3
EPS = 1e-6
N_MOD = 6

LANES = 128
SUBLANES = 8
VMEM_LIMIT_BYTES = 56 * 1024 * 1024

SEQ_TILE = 512
FF_CHUNK = 256
ATTN_KEY_CHUNK = 256
ATTN_QUERY_BLOCK = 512
ATTN_SCORES_AHEAD = 1
POOL_HALO = 16
MODS_N_TILE = 1536


def _params(sem, flags=None):
    return pltpu.CompilerParams(dimension_semantics=sem, vmem_limit_bytes=VMEM_LIMIT_BYTES,
                                flags=flags)


def _rmsnorm(x, g):
    ms = jnp.mean(x * x, axis=-1, keepdims=True)
    return x * lax.rsqrt(ms + EPS) * g


def _dot(a, b):
    return jnp.dot(a, b, preferred_element_type=F32)


def _rope_table_kernel(pos_ref, inv_ref, cc_ref, ss_ref, cct_ref, sst_ref):
    ang = inv_ref[...] * pos_ref[0]
    cos_t = jnp.cos(ang)
    sin_t = jnp.sin(ang)
    cct = jnp.concatenate([cos_t, cos_t], axis=0)
    sst = jnp.concatenate([-sin_t, sin_t], axis=0)
    cct_ref[0] = cct
    sst_ref[0] = sst
    cc_ref[0] = jnp.concatenate([cct, cct], axis=0).T
    ss_ref[0] = jnp.concatenate([sst, sst], axis=0).T


def _rope_tables(positions):
    b, s = positions.shape
    half = QK_ROPE // 2
    inv = 1.0 / (ROPE_THETA ** (jnp.arange(0, QK_ROPE, 2, dtype=F32) / QK_ROPE))
    pos = positions.astype(F32)[:, None, :]
    ts = SEQ_TILE
    tab = pl.BlockSpec((1, ts, LANES), lambda i, j: (i, j, 0))
    tab_t = pl.BlockSpec((1, QK_ROPE, ts), lambda i, j: (i, 0, j))
    return pl.pallas_call(
        _rope_table_kernel,
        grid=(b, s // ts),
        in_specs=[pl.BlockSpec((1, 1, ts), lambda i, j: (i, 0, j)),
                  pl.BlockSpec((half, 1), lambda i, j: (0, 0))],
        out_specs=[tab, tab, tab_t, tab_t],
        out_shape=[jax.ShapeDtypeStruct((b, s, LANES), F32)] * 2
                  + [jax.ShapeDtypeStruct((b, QK_ROPE, s), F32)] * 2,
        compiler_params=_params(("parallel", "parallel")),
        name="rope_tables",
    )(pos, inv[:, None])


def _mods_kernel(c_ref, w_ref, b_ref, o_ref):
    c = c_ref[...]
    sc = (c * jax.nn.sigmoid(c)).astype(BF16)
    o_ref[0] = _dot(sc, w_ref[0].astype(BF16)) + b_ref[0]


def _mods(c, mod_w, mod_b):
    depth, d, n = mod_w.shape
    b = c.shape[0]
    nt = MODS_N_TILE
    return pl.pallas_call(
        _mods_kernel,
        grid=(depth, n // nt),
        in_specs=[pl.BlockSpec((b, d), lambda l, j: (0, 0)),
                  pl.BlockSpec((1, d, nt), lambda l, j: (l, 0, j)),
                  pl.BlockSpec((1, 1, nt), lambda l, j: (l, 0, j))],
        out_specs=pl.BlockSpec((1, b, nt), lambda l, j: (l, 0, j)),
        out_shape=jax.ShapeDtypeStruct((depth, b, n), F32),
        compiler_params=_params(("parallel", "parallel")),
        name="adaln_mods",
    )(c, mod_w, mod_b.reshape(depth, 1, n))


def _pool_kernel(x_ref, mod_ref, g_ref, w_ref, pb_ref, ps_ref, o_ref, halo_ref):
    s = pl.program_id(1)
    ts = x_ref.shape[1]
    d = x_ref.shape[2]
    group = d // len(POOL_WINDOWS)

    @pl.when(s == 0)
    def _():
        halo_ref[...] = jnp.zeros(halo_ref.shape, F32)

    x = x_ref[0]
    mod = mod_ref[0]
    h = _rmsnorm(x, g_ref[...]) * (1.0 + mod[1:2]) + mod[0:1]
    t = s * ts + lax.broadcasted_iota(jnp.int32, (ts, 1), 0)
    ys = []
    for gi, w in enumerate(POOL_WINDOWS):
        lo = gi * group
        hg = h[:, lo:lo + group]
        win = jnp.concatenate([halo_ref[:, lo:lo + group], hg], axis=0)
        k = 1
        while k < w:
            win = win + pltpu.roll(win, k, axis=0)
            k *= 2
        cnt = jnp.minimum(t + 1, w).astype(F32)
        pooled = win[POOL_HALO:] / cnt - hg
        ys.append(_dot(pooled.astype(BF16), w_ref[gi]))
    y = (jnp.concatenate(ys, axis=-1) + pb_ref[...]) * ps_ref[...]
    o_ref[0] = x + mod[2:3] * y
    halo_ref[...] = h[ts - POOL_HALO:ts]


def _pool_layer(x, mods_l, g, w, pb, ps):
    b, s, d = x.shape
    ts = SEQ_TILE
    vec = pl.BlockSpec((1, d), lambda i, j: (0, 0))
    tile = pl.BlockSpec((1, ts, d), lambda i, j: (i, j, 0))
    return pl.pallas_call(
        _pool_kernel,
        grid=(b, s // ts),
        in_specs=[tile,
                  pl.BlockSpec((1, N_MOD, d), lambda i, j: (i, 0, 0)),
                  vec,
                  pl.BlockSpec(w.shape, lambda i, j: (0, 0, 0)),
                  vec, vec],
        out_specs=tile,
        out_shape=jax.ShapeDtypeStruct(x.shape, F32),
        scratch_shapes=[pltpu.VMEM((POOL_HALO, d), F32)],
        compiler_params=_params(("arbitrary", "arbitrary")),
        name="pool_mixer",
    )(x, mods_l, g, w, pb, ps)


def _shift_rows(a, k, prev_rows):
    rolled = pltpu.roll(a, k, axis=0)
    prev = pltpu.roll(prev_rows, k, axis=0)
    row = lax.broadcasted_iota(jnp.int32, prev_rows.shape, 0)
    head = jnp.where(row < k, prev, rolled[0:SUBLANES])
    return jnp.concatenate([head, rolled[SUBLANES:]], axis=0)


def _ffn_kernel(*refs, with_attn, with_final):
    it = iter(refs)
    x_ref = next(it)
    mod_ref = next(it)
    if with_attn:
        o_ref_in = next(it)
        wo_ref = next(it)
    g_ref = next(it)
    wup_ref = next(it)
    cw_ref = next(it)
    cb_ref = next(it)
    wd_ref = next(it)
    if with_final:
        fg_ref = next(it)
    out_ref = next(it)
    carry_ref = next(it)
    gate_ref = next(it)

    s = pl.program_id(1)
    ts = x_ref.shape[1]
    f = wd_ref.shape[0]
    fc = FF_CHUNK
    n_chunks = f // fc

    @pl.when(s == 0)
    def _():
        carry_ref[...] = jnp.zeros(carry_ref.shape, F32)

    x = x_ref[0]
    mod = mod_ref[0]
    if with_attn:
        x = x + mod[2:3] * _dot(o_ref_in[0], wo_ref[...])
    h = (_rmsnorm(x, g_ref[...]) * (1.0 + mod[4:5]) + mod[3:4]).astype(BF16)

    sqrt_half = math.sqrt(0.5)
    up = lambda j: (_dot(h, wup_ref[:, j * fc:(j + 1) * fc]),
                    _dot(h, wup_ref[:, f + j * fc:f + (j + 1) * fc]))
    nxt = up(0)
    for j in range(n_chunks):
        cols = slice(j * fc, (j + 1) * fc)
        a, v = nxt
        if j + 1 < n_chunks:
            nxt = up(j + 1)
        prev = carry_ref[:, cols]
        a1 = _shift_rows(a, 1, prev)
        a2 = _shift_rows(a, 2, prev)
        conv = a2 * cw_ref[0:1, cols] + a1 * cw_ref[1:2, cols] + a * cw_ref[2:3, cols] + cb_ref[:, cols]
        gate = 0.5 * conv * (1.0 + lax.erf(conv * sqrt_half))
        gate_ref[:, cols] = (gate * v).astype(BF16)
        carry_ref[:, cols] = a[ts - SUBLANES:ts]

    y = x + mod[5:6] * _dot(gate_ref[...], wd_ref[...])
    if with_final:
        y = _rmsnorm(y, fg_ref[...])
    out_ref[0] = y


def _ffn_layer(x, mods_l, g, wup, cw, cb, wd, attn=None, final_g=None):
    b, s, d = x.shape
    ts = SEQ_TILE
    f = wd.shape[0]
    assert f % FF_CHUNK == 0
    const2 = lambda i, j: (0, 0)
    vec = pl.BlockSpec((1, d), const2)
    tile = pl.BlockSpec((1, ts, d), lambda i, j: (i, j, 0))
    single = dict(pipeline_mode=pl.Buffered(1))
    in_specs = [tile, pl.BlockSpec((1, N_MOD, d), lambda i, j: (i, 0, 0))]
    args = [x, mods_l]
    if attn is not None:
        o, wo = attn
        in_specs += [pl.BlockSpec((1, ts, o.shape[2]), lambda i, j: (i, j, 0)),
                     pl.BlockSpec(wo.shape, const2, **single)]
        args += [o, wo]
    in_specs += [vec,
                 pl.BlockSpec(wup.shape, const2, **single),
                 pl.BlockSpec(cw.shape, const2),
                 pl.BlockSpec(cb.shape, const2),
                 pl.BlockSpec(wd.shape, const2, **single)]
    args += [g, wup, cw, cb, wd]
    if final_g is not None:
        in_specs.append(vec)
        args.append(final_g)
    kern = functools.partial(_ffn_kernel, with_attn=attn is not None, with_final=final_g is not None)
    return pl.pallas_call(
        kern,
        grid=(b, s // ts),
        in_specs=in_specs,
        out_specs=tile,
        out_shape=jax.ShapeDtypeStruct(x.shape, F32),
        scratch_shapes=[pltpu.VMEM((SUBLANES, f), F32),
                        pltpu.VMEM((ts, f), BF16)],
        compiler_params=_params(("arbitrary", "arbitrary")),
        name="convglu_ffn",
    )(*args)


def _kv_kernel(x_ref, g_ref, wc_ref, wr_ref, wrs_ref, cg_ref, wuk_ref, wuvt_ref, cc_ref, ss_ref,
               k_ref, vt_ref):
    xn = _rmsnorm(x_ref[0], g_ref[...]).astype(BF16)
    c_kv = _rmsnorm(_dot(xn, wc_ref[...]), cg_ref[...]).astype(BF16)
    cc = cc_ref[0][:, 0:QK_ROPE]
    ss = ss_ref[0][:, 0:QK_ROPE]
    k_rope = (_dot(xn, wr_ref[...]) * cc + _dot(xn, wrs_ref[...]) * ss).astype(BF16)
    k_nope = _dot(c_kv, wuk_ref[...])
    vt = lax.dot_general(wuvt_ref[...], c_kv, (((1,), (1,)), ((), ())), preferred_element_type=F32)
    for hd in range(N_HEADS):
        k_ref[0, hd, :, 0:QK_NOPE] = k_nope[:, hd * QK_NOPE:(hd + 1) * QK_NOPE].astype(BF16)
        k_ref[0, hd, :, QK_NOPE:QK_HEAD] = k_rope
        vt_ref[0, hd] = vt[hd * V_HEAD:(hd + 1) * V_HEAD, :].astype(BF16)


def _shared_kv(x, g, wc, wr, wrs, cg, wuk, wuvt, cc, ss):
    b, s, d = x.shape
    ts = SEQ_TILE
    const2 = lambda i, j: (0, 0)
    full = lambda a: pl.BlockSpec(a.shape, const2)
    tab = pl.BlockSpec((1, ts, LANES), lambda i, j: (i, j, 0))
    return pl.pallas_call(
        _kv_kernel,
        grid=(b, s // ts),
        in_specs=[pl.BlockSpec((1, ts, d), lambda i, j: (i, j, 0)),
                  full(g), full(wc), full(wr), full(wrs), full(cg), full(wuk), full(wuvt), tab, tab],
        out_specs=[pl.BlockSpec((1, N_HEADS, ts, QK_HEAD), lambda i, j: (i, 0, j, 0)),
                   pl.BlockSpec((1, N_HEADS, V_HEAD, ts), lambda i, j: (i, 0, 0, j))],
        out_shape=[jax.ShapeDtypeStruct((b, N_HEADS, s, QK_HEAD), BF16),
                   jax.ShapeDtypeStruct((b, N_HEADS, V_HEAD, s), BF16)],
        compiler_params=_params(("parallel", "parallel")),
        name="shared_kv",
    )(x, g, wc, wr, wrs, cg, wuk, wuvt, cc, ss)


def _q_kernel(x_ref, mod_ref, g_ref, wdq_ref, qg_ref, wuqt_ref, cct_ref, sst_ref, qt_ref):
    mod = mod_ref[0]
    h = (_rmsnorm(x_ref[0], g_ref[...]) * (1.0 + mod[1:2]) + mod[0:1]).astype(BF16)
    c_q = _rmsnorm(_dot(h, wdq_ref[...]), qg_ref[...]).astype(BF16)
    scale = QK_HEAD ** -0.5 * math.log2(math.e)
    n_nope = N_HEADS * QK_NOPE
    n_rope = N_HEADS * QK_ROPE
    qt = lax.dot_general(wuqt_ref[...], c_q, (((1,), (1,)), ((), ())), preferred_element_type=F32)
    cct = jnp.concatenate([cct_ref[0]] * N_HEADS, axis=0)
    sst = jnp.concatenate([sst_ref[0]] * N_HEADS, axis=0)
    q_nope = qt[0:n_nope] * scale
    q_rope = (qt[n_nope:n_nope + n_rope] * cct + qt[n_nope + n_rope:n_nope + 2 * n_rope] * sst) * scale
    for hd in range(N_HEADS):
        qt_ref[0, hd, 0:QK_NOPE, :] = q_nope[hd * QK_NOPE:(hd + 1) * QK_NOPE].astype(BF16)
        qt_ref[0, hd, QK_NOPE:QK_HEAD, :] = q_rope[hd * QK_ROPE:(hd + 1) * QK_ROPE].astype(BF16)


def _q_proj(x, mods_l, g, wdq, qg, wuqt, cct, sst):
    b, s, d = x.shape
    ts = SEQ_TILE
    const2 = lambda i, j: (0, 0)
    full = lambda a: pl.BlockSpec(a.shape, const2)
    tab_t = pl.BlockSpec((1, QK_ROPE, ts), lambda i, j: (i, 0, j))
    return pl.pallas_call(
        _q_kernel,
        grid=(b, s // ts),
        in_specs=[pl.BlockSpec((1, ts, d), lambda i, j: (i, j, 0)),
                  pl.BlockSpec((1, N_MOD, d), lambda i, j: (i, 0, 0)),
                  full(g), full(wdq), full(qg), full(wuqt), tab_t, tab_t],
        out_specs=pl.BlockSpec((1, N_HEADS, QK_HEAD, ts), lambda i, j: (i, 0, 0, j)),
        out_shape=jax.ShapeDtypeStruct((b, N_HEADS, QK_HEAD, s), BF16),
        compiler_params=_params(("parallel", "parallel")),
        name="q_proj",
    )(x, mods_l, g, wdq, qg, wuqt, cct, sst)


def _attn_kernel(qt_ref, k_ref, vt_ref, o_ref, acc_ref, m_ref, l_ref, s0_ref, s1_ref, s2_ref,
                 p0_ref, p1_ref):
    s_len = qt_ref.shape[3]
    tk, qb = s0_ref.shape
    s_refs = (s0_ref, s1_ref, s2_ref)
    p_refs = (p0_ref, p1_ref)

    units = []
    for d in range(s_len // tk):
        c0 = d * tk
        while c0 < s_len:
            c1 = min((c0 // qb + 1) * qb, s_len)
            units.append((d, c0, c1))
            c0 = c1

    def scores(u):
        d, c0, c1 = units[u]
        st = _dot(k_ref[0, 0, d * tk:(d + 1) * tk, :], qt_ref[0, 0, :, c0:c1])
        if c0 == d * tk:
            key = lax.broadcasted_iota(jnp.int32, (tk, tk), 0)
            qry = lax.broadcasted_iota(jnp.int32, (tk, tk), 1)
            tri = jnp.where(key <= qry, st[:, :tk], jnp.finfo(F32).min)
            st = tri if c1 - c0 == tk else jnp.concatenate([tri, st[:, tk:]], axis=1)
        s_refs[u % 3][:, :c1 - c0] = st
        return jnp.max(st, axis=0, keepdims=True)

    def softmax_step(u, st_max):
        d, c0, c1 = units[u]
        m_prev = m_ref[:, c0:c1]
        m_new = jnp.maximum(m_prev, st_max)
        alpha = jnp.exp2(m_prev - m_new)
        p = jnp.exp2(s_refs[u % 3][:, :c1 - c0] - m_new)
        p_refs[u % 2][:, :c1 - c0] = p.astype(BF16)
        l_ref[:, c0:c1] = alpha * l_ref[:, c0:c1] + jnp.sum(p, axis=0, keepdims=True)
        m_ref[:, c0:c1] = m_new
        return alpha

    def accumulate(u, alpha):
        d, c0, c1 = units[u]
        pv = _dot(vt_ref[0, 0, :, d * tk:(d + 1) * tk], p_refs[u % 2][:, :c1 - c0])
        if d == 0:
            acc_ref[:, c0:c1] = pv
        else:
            acc_ref[:, c0:c1] = alpha * acc_ref[:, c0:c1] + pv

    m_ref[...] = jnp.full(m_ref.shape, -jnp.inf, F32)
    l_ref[...] = jnp.zeros(l_ref.shape, F32)
    ahead = ATTN_SCORES_AHEAD
    maxes = {u: scores(u) for u in range(ahead)}
    alpha_prev = None
    for u in range(len(units)):
        if u + ahead < len(units):
            maxes[u + ahead] = scores(u + ahead)
        alpha = softmax_step(u, maxes.pop(u))
        if u:
            accumulate(u - 1, alpha_prev)
        alpha_prev = alpha
    accumulate(len(units) - 1, alpha_prev)
    o_ref[0] = (acc_ref[...] / l_ref[...]).T.astype(o_ref.dtype)


def _attention(qt, k, vt):
    b, nh, dk, s = qt.shape
    dv = vt.shape[2]
    tk = ATTN_KEY_CHUNK
    qb = ATTN_QUERY_BLOCK
    assert s % qb == 0 and qb % tk == 0
    return pl.pallas_call(
        _attn_kernel,
        grid=(b, nh),
        in_specs=[pl.BlockSpec((1, 1, dk, s), lambda i, h: (i, h, 0, 0)),
                  pl.BlockSpec((1, 1, s, dk), lambda i, h: (i, h, 0, 0)),
                  pl.BlockSpec((1, 1, dv, s), lambda i, h: (i, h, 0, 0))],
        out_specs=pl.BlockSpec((1, s, dv), lambda i, h: (i, 0, h)),
        out_shape=jax.ShapeDtypeStruct((b, s, nh * dv), BF16),
        scratch_shapes=[pltpu.VMEM((dv, s), F32), pltpu.VMEM((1, s), F32), pltpu.VMEM((1, s), F32),
                        *[pltpu.VMEM((tk, qb), F32)] * 3, *[pltpu.VMEM((tk, qb), BF16)] * 2],
        compiler_params=_params(("parallel", "parallel")),
        name="causal_attention",
    )(qt, k, vt)


def _swap_rope_halves(w):
    half = QK_ROPE // 2
    return jnp.concatenate([w[..., half:], w[..., :half]], axis=-1)


def _prep_wuq(w_uq):
    r = w_uq.shape[0]
    w = w_uq.reshape(r, N_HEADS, QK_HEAD)
    nope = w[:, :, :QK_NOPE].reshape(r, N_HEADS * QK_NOPE)
    rope = w[:, :, QK_NOPE:]
    rope_n = rope.reshape(r, N_HEADS * QK_ROPE)
    rope_s = _swap_rope_halves(rope).reshape(r, N_HEADS * QK_ROPE)
    return jnp.concatenate([nope, rope_n, rope_s], axis=-1).T.astype(BF16)


def kernel(x, c, positions, mod_w, mod_b, norm1_g, norm2_g, pool_w, pool_b, pool_scale, kv_in_g,
           w_dkv, ckv_norm_g, w_uk, w_uv, w_dq, q_norm_g, w_uq, w_o, w_up, conv_w, conv_b, w_down,
           final_g):
    b, s, d = x.shape
    depth = mod_w.shape[0]
    n_pool = pool_w.shape[0]
    kv_rank = ckv_norm_g.shape[0]

    cc, ss, cct, sst = _rope_tables(positions)
    mods = _mods(c, mod_w, mod_b).reshape(depth, b, N_MOD, d)
    row = lambda a: a.reshape(1, -1)

    k = vt = None
    for l in range(depth):
        ffn_w = (w_up[l].astype(BF16), conv_w[l], row(conv_b[l]), w_down[l].astype(BF16))
        final = row(final_g) if l == depth - 1 else None
        if l < n_pool:
            x = _pool_layer(x, mods[l], row(norm1_g[l]), pool_w[l].astype(BF16), row(pool_b[l]),
                            row(pool_scale[l]))
            x = _ffn_layer(x, mods[l], row(norm2_g[l]), *ffn_w, final_g=final)
        else:
            j = l - n_pool
            q = _q_proj(x, mods[l], row(norm1_g[l]), w_dq[j].astype(BF16), row(q_norm_g[j]),
                        _prep_wuq(w_uq[j]), cct, sst)
            o = _attention(q, k, vt)
            x = _ffn_layer(x, mods[l], row(norm2_g[l]), *ffn_w, attn=(o, w_o[j].astype(BF16)),
                           final_g=final)
        if l == n_pool - 1:
            w_rope = w_dkv[:, kv_rank:]
            k, vt = _shared_kv(x, row(kv_in_g), w_dkv[:, :kv_rank].astype(BF16), w_rope.astype(BF16),
                               _swap_rope_halves(w_rope).astype(BF16), row(ckv_norm_g),
                               w_uk.astype(BF16), w_uv.T.astype(BF16), cc, ss)
    return x
```

```python
import functools
import math

import jax
import jax.numpy as jnp
from jax import lax
from jax.experimental import pallas as pl
from jax.experimental.pallas import tpu as pltpu

F32 = jnp.float32
BF16 = jnp.bfloat16

POOL_WINDOWS = (2, 4, 8, 16)
N_HEADS = 8
QK_NOPE = 128
QK_ROPE = 64
V_HEAD = 128
QK_HEAD = QK_NOPE + QK_ROPE
QK_PAD = 256
V_EXT = V_HEAD + 16
ROPE_THETA = 10000.0
CONV_WIDTH = 3
EPS = 1e-6
N_MOD = 6

LANES = 128
SUBLANES = 8
VMEM_LIMIT_BYTES = 56 * 1024 * 1024

SEQ_TILE = 512
LAYER_SUBTILES = 2
FF_CHUNK = 256
ATTN_KEY_CHUNK = 256
ATTN_QUERY_BLOCK = 512
ATTN_SCORES_AHEAD = 1
POOL_HALO = 16
MODS_N_TILE = 1536


def _params(sem, flags=None):
    return pltpu.CompilerParams(dimension_semantics=sem, vmem_limit_bytes=VMEM_LIMIT_BYTES,
                                flags=flags)


def _rmsnorm(x, g):
    ms = jnp.mean(x * x, axis=-1, keepdims=True)
    return x * lax.rsqrt(ms + EPS) * g


def _dot(a, b):
    return jnp.dot(a, b, preferred_element_type=F32)


def _rope_table_kernel(pos_ref, inv_ref, cc_ref, ss_ref, cct_ref, sst_ref):
    ang = inv_ref[...] * pos_ref[0]
    cos_t = jnp.cos(ang)
    sin_t = jnp.sin(ang)
    cct = jnp.concatenate([cos_t, cos_t], axis=0)
    sst = jnp.concatenate([-sin_t, sin_t], axis=0)
    cct_ref[0] = cct
    sst_ref[0] = sst
    cc_ref[0] = jnp.concatenate([cct, cct], axis=0).T
    ss_ref[0] = jnp.concatenate([sst, sst], axis=0).T


def _rope_tables(positions):
    b, s = positions.shape
    half = QK_ROPE // 2
    inv = 1.0 / (ROPE_THETA ** (jnp.arange(0, QK_ROPE, 2, dtype=F32) / QK_ROPE))
    pos = positions.astype(F32)[:, None, :]
    ts = SEQ_TILE
    tab = pl.BlockSpec((1, ts, LANES), lambda i, j: (i, j, 0))
    tab_t = pl.BlockSpec((1, QK_ROPE, ts), lambda i, j: (i, 0, j))
    return pl.pallas_call(
        _rope_table_kernel,
        grid=(b, s // ts),
        in_specs=[pl.BlockSpec((1, 1, ts), lambda i, j: (i, 0, j)),
                  pl.BlockSpec((half, 1), lambda i, j: (0, 0))],
        out_specs=[tab, tab, tab_t, tab_t],
        out_shape=[jax.ShapeDtypeStruct((b, s, LANES), F32)] * 2
                  + [jax.ShapeDtypeStruct((b, QK_ROPE, s), F32)] * 2,
        compiler_params=_params(("parallel", "parallel")),
        name="rope_tables",
    )(pos, inv[:, None])


def _mods_kernel(c_ref, w_ref, b_ref, o_ref):
    c = c_ref[...]
    sc = (c * jax.nn.sigmoid(c)).astype(BF16)
    o_ref[0] = _dot(sc, w_ref[0].astype(BF16)) + b_ref[0]


def _mods(c, mod_w, mod_b):
    depth, d, n = mod_w.shape
    b = c.shape[0]
    nt = MODS_N_TILE
    return pl.pallas_call(
        _mods_kernel,
        grid=(depth, n // nt),
        in_specs=[pl.BlockSpec((b, d), lambda l, j: (0, 0)),
                  pl.BlockSpec((1, d, nt), lambda l, j: (l, 0, j)),
                  pl.BlockSpec((1, 1, nt), lambda l, j: (l, 0, j))],
        out_specs=pl.BlockSpec((1, b, nt), lambda l, j: (l, 0, j)),
        out_shape=jax.ShapeDtypeStruct((depth, b, n), F32),
        compiler_params=_params(("parallel", "parallel")),
        name="adaln_mods",
    )(c, mod_w, mod_b.reshape(depth, 1, n))


def _shift_rows(a, k, prev_rows):
    rolled = pltpu.roll(a, k, axis=0)
    prev = pltpu.roll(prev_rows, k, axis=0)
    row = lax.broadcasted_iota(jnp.int32, prev_rows.shape, 0)
    head = jnp.where(row < k, prev, rolled[0:SUBLANES])
    return jnp.concatenate([head, rolled[SUBLANES:]], axis=0)


def _pooled_groups(h, halo, first_pos):
    rows, d = h.shape
    group = d // len(POOL_WINDOWS)
    t = first_pos + lax.broadcasted_iota(jnp.int32, (rows, 1), 0)
    pooled = []
    for gi, w in enumerate(POOL_WINDOWS):
        lo = gi * group
        hg = h[:, lo:lo + group]
        win = jnp.concatenate([halo[:, lo:lo + group], hg], axis=0)
        k = 1
        while k < w:
            win = win + pltpu.roll(win, k, axis=0)
            k *= 2
        cnt = jnp.minimum(t + 1, w).astype(F32)
        pooled.append((win[POOL_HALO:] / cnt - hg).astype(BF16))
    return pooled


def _layer_kernel(*refs, mixer, with_final):
    it = iter(refs)
    x_ref = next(it)
    mod_ref = next(it)
    if mixer == "attn":
        o_in_ref = next(it)
        wo_ref = next(it)
    else:
        g1_ref = next(it)
        pw_ref = next(it)
        pb_ref = next(it)
        ps_ref = next(it)
    g2_ref = next(it)
    wup_ref = next(it)
    cw_ref = next(it)
    cb_ref = next(it)
    wd_ref = next(it)
    if with_final:
        fg_ref = next(it)
    out_ref = next(it)
    carry_ref = next(it)
    gate_ref = next(it)
    if mixer == "pool":
        halo_ref = next(it)

    s = pl.program_id(1)
    ts = x_ref.shape[1]
    sub = ts // LAYER_SUBTILES
    f = wd_ref.shape[0]
    fc = FF_CHUNK
    n_chunks = f // fc

    @pl.when(s == 0)
    def _():
        carry_ref[...] = jnp.zeros(carry_ref.shape, F32)
        if mixer == "pool":
            halo_ref[...] = jnp.zeros(halo_ref.shape, F32)

    mod = mod_ref[0]

    halo = halo_ref[...] if mixer == "pool" else None

    def mixer_vector_part(t, anchor=None):
        nonlocal halo
        if mixer == "attn":
            return None
        x = x_ref[0, t * sub:(t + 1) * sub]
        if anchor is not None:
            x = x + anchor
        h1 = _rmsnorm(x, g1_ref[...]) * (1.0 + mod[1:2]) + mod[0:1]
        pooled = _pooled_groups(h1, halo, s * ts + t * sub)
        halo = h1[sub - POOL_HALO:sub]
        return pooled

    def mixer_matmul_part(t, pooled):
        rows = slice(t * sub, (t + 1) * sub)
        if mixer == "attn":
            y = _dot(o_in_ref[0, rows], wo_ref[...])
        else:
            y = jnp.concatenate([_dot(p, pw_ref[gi]) for gi, p in enumerate(pooled)], axis=-1)
            y = (y + pb_ref[...]) * ps_ref[...]
        x1 = x_ref[0, rows] + mod[2:3] * y
        out_ref[0, rows] = x1
        return (_rmsnorm(x1, g2_ref[...]) * (1.0 + mod[4:5]) + mod[3:4]).astype(BF16)

    def zero_after(value):
        bits = lax.bitcast_convert_type(value[0:1, 0:LANES], jnp.uint32)
        bits = lax.shift_right_logical(lax.shift_right_logical(bits, jnp.uint32(16)), jnp.uint32(16))
        zeros = lax.bitcast_convert_type(bits, F32)
        return jnp.concatenate([zeros] * (x_ref.shape[2] // LANES), axis=1)

    sqrt_half = math.sqrt(0.5)
    tails = [carry_ref[:, j * fc:(j + 1) * fc] for j in range(n_chunks)]
    h_next = mixer_matmul_part(0, mixer_vector_part(0))
    for t in range(LAYER_SUBTILES):
        rows = slice(t * sub, (t + 1) * sub)
        h = h_next
        up = lambda j: (_dot(h, wup_ref[:, j * fc:(j + 1) * fc]),
                        _dot(h, wup_ref[:, f + j * fc:f + (j + 1) * fc]))
        nxt = up(0)
        if t + 1 < LAYER_SUBTILES:
            pooled_next = mixer_vector_part(t + 1, anchor=zero_after(nxt[0]))
        for j in range(n_chunks):
            cols = slice(j * fc, (j + 1) * fc)
            a, v = nxt
            if j + 1 < n_chunks:
                nxt = up(j + 1)
            a1 = _shift_rows(a, 1, tails[j])
            a2 = _shift_rows(a, 2, tails[j])
            conv = (a2 * cw_ref[0:1, cols] + a1 * cw_ref[1:2, cols] + a * cw_ref[2:3, cols]
                    + cb_ref[:, cols])
            gate = 0.5 * conv * (1.0 + lax.erf(conv * sqrt_half))
            gate_ref[:, cols] = (gate * v).astype(BF16)
            tails[j] = a[sub - SUBLANES:sub]
        if t + 1 < LAYER_SUBTILES:
            h_next = mixer_matmul_part(t + 1, pooled_next)
        y = out_ref[0, rows] + mod[5:6] * _dot(gate_ref[...], wd_ref[...])
        if with_final:
            y = _rmsnorm(y, fg_ref[...])
        out_ref[0, rows] = y
    for j in range(n_chunks):
        carry_ref[:, j * fc:(j + 1) * fc] = tails[j]
    if mixer == "pool":
        halo_ref[...] = halo


def _layer_tail(x, mods_l, g2, wup, cw, cb, wd, attn=None, pool=None, final_g=None):
    b, s, d = x.shape
    ts = LAYER_SUBTILES * SEQ_TILE
    f = wd.shape[0]
    assert f % FF_CHUNK == 0 and s % ts == 0 and (attn is None) != (pool is None)
    const2 = lambda i, j: (0, 0)
    vec = pl.BlockSpec((1, d), const2)
    tile = pl.BlockSpec((1, ts, d), lambda i, j: (i, j, 0))
    single = dict(pipeline_mode=pl.Buffered(1))
    in_specs = [tile, pl.BlockSpec((1, N_MOD, d), lambda i, j: (i, 0, 0))]
    args = [x, mods_l]
    scratch = [pltpu.VMEM((SUBLANES, f), F32), pltpu.VMEM((SEQ_TILE, f), BF16)]
    if attn is not None:
        o, wo = attn
        in_specs += [pl.BlockSpec((1, ts, o.shape[2]), lambda i, j: (i, j, 0)),
                     pl.BlockSpec(wo.shape, const2, **single)]
        args += [o, wo]
    else:
        g1, pw, pb, ps = pool
        in_specs += [vec, pl.BlockSpec(pw.shape, lambda i, j: (0, 0, 0)), vec, vec]
        args += [g1, pw, pb, ps]
        scratch.append(pltpu.VMEM((POOL_HALO, d), F32))
    in_specs += [vec,
                 pl.BlockSpec(wup.shape, const2, **single),
                 pl.BlockSpec(cw.shape, const2),
                 pl.BlockSpec(cb.shape, const2),
                 pl.BlockSpec(wd.shape, const2, **single)]
    args += [g2, wup, cw, cb, wd]
    if final_g is not None:
        in_specs.append(vec)
        args.append(final_g)
    kern = functools.partial(_layer_kernel, mixer="attn" if attn is not None else "pool",
                             with_final=final_g is not None)
    return pl.pallas_call(
        kern,
        grid=(b, s // ts),
        in_specs=in_specs,
        out_specs=tile,
        out_shape=jax.ShapeDtypeStruct(x.shape, F32),
        scratch_shapes=scratch,
        compiler_params=_params(("arbitrary", "arbitrary")),
        name="layer_tail_" + ("attn" if attn is not None else "pool"),
    )(*args)


def _kv_kernel(x_ref, g_ref, wc_ref, wr_ref, wrs_ref, cg_ref, wuk_ref, wuvt_ref, cc_ref, ss_ref,
               k_ref, vt_ref):
    ts = x_ref.shape[1]
    xn = _rmsnorm(x_ref[0], g_ref[...]).astype(BF16)
    c_kv = _rmsnorm(_dot(xn, wc_ref[...]), cg_ref[...]).astype(BF16)
    row = lax.broadcasted_iota(jnp.int32, (V_EXT - V_HEAD, ts), 0)
    ones_row = jnp.where(row == 0, 1.0, 0.0).astype(BF16)
    cc = cc_ref[0][:, 0:QK_ROPE]
    ss = ss_ref[0][:, 0:QK_ROPE]
    k_rope = (_dot(xn, wr_ref[...]) * cc + _dot(xn, wrs_ref[...]) * ss).astype(BF16)
    k_nope = _dot(c_kv, wuk_ref[...])
    vt = lax.dot_general(wuvt_ref[...], c_kv, (((1,), (1,)), ((), ())), preferred_element_type=F32)
    for hd in range(N_HEADS):
        k_ref[0, hd, :, 0:QK_NOPE] = k_nope[:, hd * QK_NOPE:(hd + 1) * QK_NOPE].astype(BF16)
        k_ref[0, hd, :, QK_NOPE:QK_HEAD] = k_rope
        k_ref[0, hd, :, QK_HEAD:QK_PAD] = jnp.zeros((ts, QK_PAD - QK_HEAD), BF16)
        vt_ref[0, hd, 0:V_HEAD, :] = vt[hd * V_HEAD:(hd + 1) * V_HEAD, :].astype(BF16)
        vt_ref[0, hd, V_HEAD:V_EXT, :] = ones_row


def _shared_kv(x, g, wc, wr, wrs, cg, wuk, wuvt, cc, ss):
    b, s, d = x.shape
    ts = SEQ_TILE
    const2 = lambda i, j: (0, 0)
    full = lambda a: pl.BlockSpec(a.shape, const2)
    tab = pl.BlockSpec((1, ts, LANES), lambda i, j: (i, j, 0))
    return pl.pallas_call(
        _kv_kernel,
        grid=(b, s // ts),
        in_specs=[pl.BlockSpec((1, ts, d), lambda i, j: (i, j, 0)),
                  full(g), full(wc), full(wr), full(wrs), full(cg), full(wuk), full(wuvt), tab, tab],
        out_specs=[pl.BlockSpec((1, N_HEADS, ts, QK_PAD), lambda i, j: (i, 0, j, 0)),
                   pl.BlockSpec((1, N_HEADS, V_EXT, ts), lambda i, j: (i, 0, 0, j))],
        out_shape=[jax.ShapeDtypeStruct((b, N_HEADS, s, QK_PAD), BF16),
                   jax.ShapeDtypeStruct((b, N_HEADS, V_EXT, s), BF16)],
        compiler_params=_params(("parallel", "parallel")),
        name="shared_kv",
    )(x, g, wc, wr, wrs, cg, wuk, wuvt, cc, ss)


def _q_kernel(x_ref, mod_ref, g_ref, wdq_ref, qg_ref, wuqt_ref, cct_ref, sst_ref, qt_ref):
    mod = mod_ref[0]
    h = (_rmsnorm(x_ref[0], g_ref[...]) * (1.0 + mod[1:2]) + mod[0:1]).astype(BF16)
    c_q = _rmsnorm(_dot(h, wdq_ref[...]), qg_ref[...]).astype(BF16)
    scale = QK_HEAD ** -0.5 * math.log2(math.e)
    n_nope = N_HEADS * QK_NOPE
    n_rope = N_HEADS * QK_ROPE
    qt = lax.dot_general(wuqt_ref[...], c_q, (((1,), (1,)), ((), ())), preferred_element_type=F32)
    cct = jnp.concatenate([cct_ref[0]] * N_HEADS, axis=0)
    sst = jnp.concatenate([sst_ref[0]] * N_HEADS, axis=0)
    q_nope = qt[0:n_nope] * scale
    q_rope = (qt[n_nope:n_nope + n_rope] * cct + qt[n_nope + n_rope:n_nope + 2 * n_rope] * sst) * scale
    for hd in range(N_HEADS):
        qt_ref[0, hd, 0:QK_NOPE, :] = q_nope[hd * QK_NOPE:(hd + 1) * QK_NOPE].astype(BF16)
        qt_ref[0, hd, QK_NOPE:QK_HEAD, :] = q_rope[hd * QK_ROPE:(hd + 1) * QK_ROPE].astype(BF16)
        qt_ref[0, hd, QK_HEAD:QK_PAD, :] = jnp.zeros((QK_PAD - QK_HEAD, qt.shape[1]), BF16)


def _q_proj(x, mods_l, g, wdq, qg, wuqt, cct, sst):
    b, s, d = x.shape
    ts = SEQ_TILE
    const2 = lambda i, j: (0, 0)
    full = lambda a: pl.BlockSpec(a.shape, const2)
    tab_t = pl.BlockSpec((1, QK_ROPE, ts), lambda i, j: (i, 0, j))
    return pl.pallas_call(
        _q_kernel,
        grid=(b, s // ts),
        in_specs=[pl.BlockSpec((1, ts, d), lambda i, j: (i, j, 0)),
                  pl.BlockSpec((1, N_MOD, d), lambda i, j: (i, 0, 0)),
                  full(g), full(wdq), full(qg), full(wuqt), tab_t, tab_t],
        out_specs=pl.BlockSpec((1, N_HEADS, QK_PAD, ts), lambda i, j: (i, 0, 0, j)),
        out_shape=jax.ShapeDtypeStruct((b, N_HEADS, QK_PAD, s), BF16),
        compiler_params=_params(("parallel", "parallel")),
        name="q_proj",
    )(x, mods_l, g, wdq, qg, wuqt, cct, sst)


def _attn_kernel(qt_ref, k_ref, vt_ref, o_ref, acc_ref, m_ref, s0_ref, s1_ref, s2_ref,
                 p0_ref, p1_ref):
    s_len = qt_ref.shape[3]
    tk, qb = s0_ref.shape
    s_refs = (s0_ref, s1_ref, s2_ref)
    p_refs = (p0_ref, p1_ref)

    units = []
    for d in range(s_len // tk):
        c0 = d * tk
        while c0 < s_len:
            c1 = min((c0 // qb + 1) * qb, s_len)
            units.append((d, c0, c1))
            c0 = c1

    def scores(u):
        d, c0, c1 = units[u]
        st = _dot(k_ref[0, 0, d * tk:(d + 1) * tk, :], qt_ref[0, 0, :, c0:c1])
        if c0 == d * tk:
            key = lax.broadcasted_iota(jnp.int32, (tk, tk), 0)
            qry = lax.broadcasted_iota(jnp.int32, (tk, tk), 1)
            tri = jnp.where(key <= qry, st[:, :tk], jnp.finfo(F32).min)
            st = tri if c1 - c0 == tk else jnp.concatenate([tri, st[:, tk:]], axis=1)
        s_refs[u % 3][:, :c1 - c0] = st
        return jnp.max(st, axis=0, keepdims=True)

    def softmax_step(u, st_max):
        d, c0, c1 = units[u]
        m_prev = m_ref[:, c0:c1]
        m_new = jnp.maximum(m_prev, st_max)
        alpha = jnp.exp2(m_prev - m_new)
        p = jnp.exp2(s_refs[u % 3][:, :c1 - c0] - m_new)
        p_refs[u % 2][:, :c1 - c0] = p.astype(BF16)
        m_ref[:, c0:c1] = m_new
        return alpha

    def accumulate(u, alpha):
        d, c0, c1 = units[u]
        pv = _dot(vt_ref[0, 0, :, d * tk:(d + 1) * tk], p_refs[u % 2][:, :c1 - c0])
        if d == 0:
            acc_ref[:, c0:c1] = pv
        else:
            acc_ref[:, c0:c1] = alpha * acc_ref[:, c0:c1] + pv

    m_ref[...] = jnp.full(m_ref.shape, -jnp.inf, F32)
    ahead = ATTN_SCORES_AHEAD
    maxes = {u: scores(u) for u in range(ahead)}
    alpha_prev = None
    for u in range(len(units)):
        if u + ahead < len(units):
            maxes[u + ahead] = scores(u + ahead)
        alpha = softmax_step(u, maxes.pop(u))
        if u:
            accumulate(u - 1, alpha_prev)
        alpha_prev = alpha
    accumulate(len(units) - 1, alpha_prev)
    o_ref[0] = (acc_ref[0:V_HEAD, :] / acc_ref[V_HEAD:V_HEAD + 1, :]).T.astype(o_ref.dtype)


def _attention(qt, k, vt):
    b, nh, dk, s = qt.shape
    dv = vt.shape[2]
    assert dv == V_EXT
    tk = ATTN_KEY_CHUNK
    qb = ATTN_QUERY_BLOCK
    assert s % qb == 0 and qb % tk == 0
    return pl.pallas_call(
        _attn_kernel,
        grid=(b, nh),
        in_specs=[pl.BlockSpec((1, 1, dk, s), lambda i, h: (i, h, 0, 0)),
                  pl.BlockSpec((1, 1, s, dk), lambda i, h: (i, h, 0, 0)),
                  pl.BlockSpec((1, 1, dv, s), lambda i, h: (i, h, 0, 0))],
        out_specs=pl.BlockSpec((1, s, V_HEAD), lambda i, h: (i, 0, h)),
        out_shape=jax.ShapeDtypeStruct((b, s, nh * V_HEAD), BF16),
        scratch_shapes=[pltpu.VMEM((dv, s), F32), pltpu.VMEM((1, s), F32),
                        *[pltpu.VMEM((tk, qb), F32)] * 3, *[pltpu.VMEM((tk, qb), BF16)] * 2],
        compiler_params=_params(("parallel", "parallel")),
        name="causal_attention",
    )(qt, k, vt)


def _swap_rope_halves(w):
    half = QK_ROPE // 2
    return jnp.concatenate([w[..., half:], w[..., :half]], axis=-1)


def _prep_wuq(w_uq):
    r = w_uq.shape[0]
    w = w_uq.reshape(r, N_HEADS, QK_HEAD)
    nope = w[:, :, :QK_NOPE].reshape(r, N_HEADS * QK_NOPE)
    rope = w[:, :, QK_NOPE:]
    rope_n = rope.reshape(r, N_HEADS * QK_ROPE)
    rope_s = _swap_rope_halves(rope).reshape(r, N_HEADS * QK_ROPE)
    return jnp.concatenate([nope, rope_n, rope_s], axis=-1).T.astype(BF16)


def kernel(x, c, positions, mod_w, mod_b, norm1_g, norm2_g, pool_w, pool_b, pool_scale, kv_in_g,
           w_dkv, ckv_norm_g, w_uk, w_uv, w_dq, q_norm_g, w_uq, w_o, w_up, conv_w, conv_b, w_down,
           final_g):
    b, s, d = x.shape
    depth = mod_w.shape[0]
    n_pool = pool_w.shape[0]
    kv_rank = ckv_norm_g.shape[0]

    cc, ss, cct, sst = _rope_tables(positions)
    mods = _mods(c, mod_w, mod_b).reshape(depth, b, N_MOD, d)
    row = lambda a: a.reshape(1, -1)

    k = vt = None
    for l in range(depth):
        ffn_w = (w_up[l].astype(BF16), conv_w[l], row(conv_b[l]), w_down[l].astype(BF16))
        final = row(final_g) if l == depth - 1 else None
        if l < n_pool:
            pool = (row(norm1_g[l]), pool_w[l].astype(BF16), row(pool_b[l]), row(pool_scale[l]))
            x = _layer_tail(x, mods[l], row(norm2_g[l]), *ffn_w, pool=pool, final_g=final)
        else:
            j = l - n_pool
            q = _q_proj(x, mods[l], row(norm1_g[l]), w_dq[j].astype(BF16), row(q_norm_g[j]),
                        _prep_wuq(w_uq[j]), cct, sst)
            o = _attention(q, k, vt)
            x = _layer_tail(x, mods[l], row(norm2_g[l]), *ffn_w, attn=(o, w_o[j].astype(BF16)),
                            final_g=final)
        if l == n_pool - 1:
            w_rope = w_dkv[:, kv_rank:]
            k, vt = _shared_kv(x, row(kv_in_g), w_dkv[:, :kv_rank].astype(BF16), w_rope.astype(BF16),
                               _swap_rope_halves(w_rope).astype(BF16), row(ckv_norm_g),
                               w_uk.astype(BF16), w_uv.T.astype(BF16), cc, ss)
    return x
```

```python
import functools
import math

import jax
import jax.numpy as jnp
from jax import lax
from jax.experimental import pallas as pl
from jax.experimental.pallas import tpu as pltpu

F32 = jnp.float32
BF16 = jnp.bfloat16

POOL_WINDOWS = (2, 4, 8, 16)
N_HEADS = 8
QK_NOPE = 128
QK_ROPE = 64
V_HEAD = 128
QK_HEAD = QK_NOPE + QK_ROPE
QK_PAD = 256
V_EXT = V_HEAD + 16
ROPE_THETA = 10000.0
CONV_WIDTH = 3
EPS = 1e-6
N_MOD = 6

LANES = 128
SUBLANES = 8
VMEM_LIMIT_BYTES = 56 * 1024 * 1024

SEQ_TILE = 512
LAYER_SUBTILES = 2
FF_CHUNK = 256
ATTN_KEY_CHUNK = 256
ATTN_QUERY_BLOCK = 512
ATTN_SCORES_AHEAD = 1
POOL_HALO = 16
MODS_N_TILE = 1536


def _params(sem, flags=None):
    return pltpu.CompilerParams(dimension_semantics=sem, vmem_limit_bytes=VMEM_LIMIT_BYTES,
                                flags=flags)


def _rmsnorm(x, g):
    ms = jnp.mean(x * x, axis=-1, keepdims=True)
    return x * lax.rsqrt(ms + EPS) * g


def _dot(a, b):
    return jnp.dot(a, b, preferred_element_type=F32)


def _rope_table_kernel(pos_ref, inv_ref, cc_ref, ss_ref, cct_ref, sst_ref):
    ang = inv_ref[...] * pos_ref[0]
    cos_t = jnp.cos(ang)
    sin_t = jnp.sin(ang)
    cct = jnp.concatenate([cos_t, cos_t], axis=0)
    sst = jnp.concatenate([-sin_t, sin_t], axis=0)
    cct_ref[0] = cct
    sst_ref[0] = sst
    cc_ref[0] = jnp.concatenate([cct, cct], axis=0).T
    ss_ref[0] = jnp.concatenate([sst, sst], axis=0).T


def _rope_tables(positions):
    b, s = positions.shape
    half = QK_ROPE // 2
    inv = 1.0 / (ROPE_THETA ** (jnp.arange(0, QK_ROPE, 2, dtype=F32) / QK_ROPE))
    pos = positions.astype(F32)[:, None, :]
    ts = SEQ_TILE
    tab = pl.BlockSpec((1, ts, LANES), lambda i, j: (i, j, 0))
    tab_t = pl.BlockSpec((1, QK_ROPE, ts), lambda i, j: (i, 0, j))
    return pl.pallas_call(
        _rope_table_kernel,
        grid=(b, s // ts),
        in_specs=[pl.BlockSpec((1, 1, ts), lambda i, j: (i, 0, j)),
                  pl.BlockSpec((half, 1), lambda i, j: (0, 0))],
        out_specs=[tab, tab, tab_t, tab_t],
        out_shape=[jax.ShapeDtypeStruct((b, s, LANES), F32)] * 2
                  + [jax.ShapeDtypeStruct((b, QK_ROPE, s), F32)] * 2,
        compiler_params=_params(("parallel", "parallel")),
        name="rope_tables",
    )(pos, inv[:, None])


def _mods_kernel(c_ref, w_ref, b_ref, o_ref):
    c = c_ref[...]
    sc = (c * jax.nn.sigmoid(c)).astype(BF16)
    o_ref[0] = _dot(sc, w_ref[0].astype(BF16)) + b_ref[0]


def _mods(c, mod_w, mod_b):
    depth, d, n = mod_w.shape
    b = c.shape[0]
    nt = MODS_N_TILE
    return pl.pallas_call(
        _mods_kernel,
        grid=(depth, n // nt),
        in_specs=[pl.BlockSpec((b, d), lambda l, j: (0, 0)),
                  pl.BlockSpec((1, d, nt), lambda l, j: (l, 0, j)),
                  pl.BlockSpec((1, 1, nt), lambda l, j: (l, 0, j))],
        out_specs=pl.BlockSpec((1, b, nt), lambda l, j: (l, 0, j)),
        out_shape=jax.ShapeDtypeStruct((depth, b, n), F32),
        compiler_params=_params(("parallel", "parallel")),
        name="adaln_mods",
    )(c, mod_w, mod_b.reshape(depth, 1, n))


def _shift_rows(a, k, prev_rows):
    rolled = pltpu.roll(a, k, axis=0)
    prev = pltpu.roll(prev_rows, k, axis=0)
    row = lax.broadcasted_iota(jnp.int32, prev_rows.shape, 0)
    head = jnp.where(row < k, prev, rolled[0:SUBLANES])
    return jnp.concatenate([head, rolled[SUBLANES:]], axis=0)


def _pooled_groups(h, halo, first_pos):
    rows, d = h.shape
    group = d // len(POOL_WINDOWS)
    t = first_pos + lax.broadcasted_iota(jnp.int32, (rows, 1), 0)
    pooled = []
    for gi, w in enumerate(POOL_WINDOWS):
        lo = gi * group
        hg = h[:, lo:lo + group]
        win = jnp.concatenate([halo[:, lo:lo + group], hg], axis=0)
        k = 1
        while k < w:
            win = win + pltpu.roll(win, k, axis=0)
            k *= 2
        cnt = jnp.minimum(t + 1, w).astype(F32)
        pooled.append((win[POOL_HALO:] / cnt - hg).astype(BF16))
    return pooled


def _layer_kernel(*refs, mixer, with_final):
    it = iter(refs)
    x_ref = next(it)
    mod_ref = next(it)
    if mixer == "attn":
        o_in_ref = next(it)
        wo_ref = next(it)
    else:
        g1_ref = next(it)
        pw_ref = next(it)
        pb_ref = next(it)
        ps_ref = next(it)
    g2_ref = next(it)
    wup_ref = next(it)
    cw_ref = next(it)
    cb_ref = next(it)
    wd_ref = next(it)
    if with_final:
        fg_ref = next(it)
    out_ref = next(it)
    carry_ref = next(it)
    gate_ref = next(it)
    if mixer == "pool":
        halo_ref = next(it)

    s = pl.program_id(1)
    ts = x_ref.shape[1]
    sub = ts // LAYER_SUBTILES
    f = wd_ref.shape[0]
    fc = FF_CHUNK
    n_chunks = f // fc

    @pl.when(s == 0)
    def _():
        carry_ref[...] = jnp.zeros(carry_ref.shape, F32)
        if mixer == "pool":
            halo_ref[...] = jnp.zeros(halo_ref.shape, F32)

    mod = mod_ref[0]

    halo = halo_ref[...] if mixer == "pool" else None

    def mixer_vector_part(t, anchor=None):
        nonlocal halo
        if mixer == "attn":
            return None
        x = x_ref[0, t * sub:(t + 1) * sub]
        if anchor is not None:
            x = x + anchor
        h1 = _rmsnorm(x, g1_ref[...]) * (1.0 + mod[1:2]) + mod[0:1]
        pooled = _pooled_groups(h1, halo, s * ts + t * sub)
        halo = h1[sub - POOL_HALO:sub]
        return pooled

    def mixer_matmul_part(t, pooled):
        rows = slice(t * sub, (t + 1) * sub)
        if mixer == "attn":
            o = jnp.concatenate([o_in_ref[0, hd, rows, :] for hd in range(N_HEADS)], axis=-1)
            y = _dot(o, wo_ref[...])
        else:
            y = jnp.concatenate([_dot(p, pw_ref[gi]) for gi, p in enumerate(pooled)], axis=-1)
            y = (y + pb_ref[...]) * ps_ref[...]
        x1 = x_ref[0, rows] + mod[2:3] * y
        out_ref[0, rows] = x1
        return (_rmsnorm(x1, g2_ref[...]) * (1.0 + mod[4:5]) + mod[3:4]).astype(BF16)

    def zero_after(value):
        bits = lax.bitcast_convert_type(value[0:1, 0:LANES], jnp.uint32)
        bits = lax.shift_right_logical(lax.shift_right_logical(bits, jnp.uint32(16)), jnp.uint32(16))
        zeros = lax.bitcast_convert_type(bits, F32)
        return jnp.concatenate([zeros] * (x_ref.shape[2] // LANES), axis=1)

    sqrt_half = math.sqrt(0.5)
    tails = [carry_ref[:, j * fc:(j + 1) * fc] for j in range(n_chunks)]
    h_next = mixer_matmul_part(0, mixer_vector_part(0))
    for t in range(LAYER_SUBTILES):
        rows = slice(t * sub, (t + 1) * sub)
        h = h_next
        up = lambda j: (_dot(h, wup_ref[:, j * fc:(j + 1) * fc]),
                        _dot(h, wup_ref[:, f + j * fc:f + (j + 1) * fc]))
        nxt = up(0)
        if t + 1 < LAYER_SUBTILES:
            pooled_next = mixer_vector_part(t + 1, anchor=zero_after(nxt[0]))
        for j in range(n_chunks):
            cols = slice(j * fc, (j + 1) * fc)
            a, v = nxt
            if j + 1 < n_chunks:
                nxt = up(j + 1)
            a1 = _shift_rows(a, 1, tails[j])
            a2 = _shift_rows(a, 2, tails[j])
            conv = (a2 * cw_ref[0:1, cols] + a1 * cw_ref[1:2, cols] + a * cw_ref[2:3, cols]
                    + cb_ref[:, cols])
            gate = 0.5 * conv * (1.0 + lax.erf(conv * sqrt_half))
            gate_ref[:, cols] = (gate * v).astype(BF16)
            tails[j] = a[sub - SUBLANES:sub]
        if t + 1 < LAYER_SUBTILES:
            h_next = mixer_matmul_part(t + 1, pooled_next)
        y = out_ref[0, rows] + mod[5:6] * _dot(gate_ref[...], wd_ref[...])
        if with_final:
            y = _rmsnorm(y, fg_ref[...])
        out_ref[0, rows] = y
    for j in range(n_chunks):
        carry_ref[:, j * fc:(j + 1) * fc] = tails[j]
    if mixer == "pool":
        halo_ref[...] = halo


def _layer_tail(x, mods_l, g2, wup, cw, cb, wd, attn=None, pool=None, final_g=None):
    b, s, d = x.shape
    ts = LAYER_SUBTILES * SEQ_TILE
    f = wd.shape[0]
    assert f % FF_CHUNK == 0 and s % ts == 0 and (attn is None) != (pool is None)
    const2 = lambda i, j: (0, 0)
    vec = pl.BlockSpec((1, d), const2)
    tile = pl.BlockSpec((1, ts, d), lambda i, j: (i, j, 0))
    single = dict(pipeline_mode=pl.Buffered(1))
    in_specs = [tile, pl.BlockSpec((1, N_MOD, d), lambda i, j: (i, 0, 0))]
    args = [x, mods_l]
    scratch = [pltpu.VMEM((SUBLANES, f), F32), pltpu.VMEM((SEQ_TILE, f), BF16)]
    if attn is not None:
        o, wo = attn
        in_specs += [pl.BlockSpec((1, o.shape[1], ts, o.shape[3]), lambda i, j: (i, 0, j, 0)),
                     pl.BlockSpec(wo.shape, const2, **single)]
        args += [o, wo]
    else:
        g1, pw, pb, ps = pool
        in_specs += [vec, pl.BlockSpec(pw.shape, lambda i, j: (0, 0, 0)), vec, vec]
        args += [g1, pw, pb, ps]
        scratch.append(pltpu.VMEM((POOL_HALO, d), F32))
    in_specs += [vec,
                 pl.BlockSpec(wup.shape, const2, **single),
                 pl.BlockSpec(cw.shape, const2),
                 pl.BlockSpec(cb.shape, const2),
                 pl.BlockSpec(wd.shape, const2, **single)]
    args += [g2, wup, cw, cb, wd]
    if final_g is not None:
        in_specs.append(vec)
        args.append(final_g)
    kern = functools.partial(_layer_kernel, mixer="attn" if attn is not None else "pool",
                             with_final=final_g is not None)
    return pl.pallas_call(
        kern,
        grid=(b, s // ts),
        in_specs=in_specs,
        out_specs=tile,
        out_shape=jax.ShapeDtypeStruct(x.shape, F32),
        scratch_shapes=scratch,
        compiler_params=_params(("arbitrary", "arbitrary")),
        name="layer_tail_" + ("attn" if attn is not None else "pool"),
    )(*args)


def _kv_kernel(x_ref, g_ref, wdkv_ref, cg_ref, wuk_ref, wuvt_ref, cc_ref, ss_ref, k_ref, vt_ref):
    ts = x_ref.shape[1]
    kv_rank = cg_ref.shape[1]
    xn = _rmsnorm(x_ref[0], g_ref[...]).astype(BF16)
    kv = _dot(xn, wdkv_ref[...])
    c_kv = _rmsnorm(kv[:, 0:kv_rank], cg_ref[...]).astype(BF16)
    kr = kv[:, kv_rank:kv_rank + LANES]
    k_rope = kr * cc_ref[0] + pltpu.roll(kr, QK_ROPE // 2, axis=1) * ss_ref[0]
    k_rope = k_rope[:, 0:QK_ROPE].astype(BF16)
    row = lax.broadcasted_iota(jnp.int32, (V_EXT - V_HEAD, ts), 0)
    ones_row = jnp.where(row == 0, 1.0, 0.0).astype(BF16)
    k_nope = _dot(c_kv, wuk_ref[...])
    vt = lax.dot_general(wuvt_ref[...], c_kv, (((1,), (1,)), ((), ())), preferred_element_type=F32)
    for hd in range(N_HEADS):
        k_ref[0, hd, :, 0:QK_NOPE] = k_nope[:, hd * QK_NOPE:(hd + 1) * QK_NOPE].astype(BF16)
        k_ref[0, hd, :, QK_NOPE:QK_HEAD] = k_rope
        k_ref[0, hd, :, QK_HEAD:QK_PAD] = jnp.zeros((ts, QK_PAD - QK_HEAD), BF16)
        vt_ref[0, hd, 0:V_HEAD, :] = vt[hd * V_HEAD:(hd + 1) * V_HEAD, :].astype(BF16)
        vt_ref[0, hd, V_HEAD:V_EXT, :] = ones_row


def _shared_kv(x, g, wdkv, cg, wuk, wuvt, cc, ss):
    b, s, d = x.shape
    ts = SEQ_TILE
    const2 = lambda i, j: (0, 0)
    full = lambda a: pl.BlockSpec(a.shape, const2)
    tab = pl.BlockSpec((1, ts, LANES), lambda i, j: (i, j, 0))
    return pl.pallas_call(
        _kv_kernel,
        grid=(b, s // ts),
        in_specs=[pl.BlockSpec((1, ts, d), lambda i, j: (i, j, 0)),
                  full(g), full(wdkv), full(cg), full(wuk), full(wuvt), tab, tab],
        out_specs=[pl.BlockSpec((1, N_HEADS, ts, QK_PAD), lambda i, j: (i, 0, j, 0)),
                   pl.BlockSpec((1, N_HEADS, V_EXT, ts), lambda i, j: (i, 0, 0, j))],
        out_shape=[jax.ShapeDtypeStruct((b, N_HEADS, s, QK_PAD), BF16),
                   jax.ShapeDtypeStruct((b, N_HEADS, V_EXT, s), BF16)],
        compiler_params=_params(("parallel", "parallel")),
        name="shared_kv",
    )(x, g, wdkv, cg, wuk, wuvt, cc, ss)


def _q_kernel(x_ref, mod_ref, g_ref, wdq_ref, qg_ref, wuqt_ref, cct_ref, sst_ref, qt_ref):
    mod = mod_ref[0]
    h = (_rmsnorm(x_ref[0], g_ref[...]) * (1.0 + mod[1:2]) + mod[0:1]).astype(BF16)
    c_q = _rmsnorm(_dot(h, wdq_ref[...]), qg_ref[...]).astype(BF16)
    scale = QK_HEAD ** -0.5 * math.log2(math.e)
    n_nope = N_HEADS * QK_NOPE
    n_rope = N_HEADS * QK_ROPE
    qt = lax.dot_general(wuqt_ref[...], c_q, (((1,), (1,)), ((), ())), preferred_element_type=F32)
    cct = jnp.concatenate([cct_ref[0]] * N_HEADS, axis=0)
    sst = jnp.concatenate([sst_ref[0]] * N_HEADS, axis=0)
    q_nope = qt[0:n_nope] * scale
    qr = qt[n_nope:n_nope + n_rope]
    half = QK_ROPE // 2
    qr_swapped = jnp.concatenate(
        [qr[hd * QK_ROPE + off:hd * QK_ROPE + off + half] for hd in range(N_HEADS) for off in (half, 0)],
        axis=0)
    q_rope = (qr * cct + qr_swapped * sst) * scale
    for hd in range(N_HEADS):
        qt_ref[0, hd, 0:QK_NOPE, :] = q_nope[hd * QK_NOPE:(hd + 1) * QK_NOPE].astype(BF16)
        qt_ref[0, hd, QK_NOPE:QK_HEAD, :] = q_rope[hd * QK_ROPE:(hd + 1) * QK_ROPE].astype(BF16)
        qt_ref[0, hd, QK_HEAD:QK_PAD, :] = jnp.zeros((QK_PAD - QK_HEAD, qt.shape[1]), BF16)


def _q_proj(x, mods_l, g, wdq, qg, wuqt, cct, sst):
    b, s, d = x.shape
    ts = SEQ_TILE
    const2 = lambda i, j: (0, 0)
    full = lambda a: pl.BlockSpec(a.shape, const2)
    tab_t = pl.BlockSpec((1, QK_ROPE, ts), lambda i, j: (i, 0, j))
    return pl.pallas_call(
        _q_kernel,
        grid=(b, s // ts),
        in_specs=[pl.BlockSpec((1, ts, d), lambda i, j: (i, j, 0)),
                  pl.BlockSpec((1, N_MOD, d), lambda i, j: (i, 0, 0)),
                  full(g), full(wdq), full(qg), full(wuqt), tab_t, tab_t],
        out_specs=pl.BlockSpec((1, N_HEADS, QK_PAD, ts), lambda i, j: (i, 0, 0, j)),
        out_shape=jax.ShapeDtypeStruct((b, N_HEADS, QK_PAD, s), BF16),
        compiler_params=_params(("parallel", "parallel")),
        name="q_proj",
    )(x, mods_l, g, wdq, qg, wuqt, cct, sst)


def _attn_kernel(qt_ref, k_ref, vt_ref, o_ref, acc_ref, m_ref, s0_ref, s1_ref, s2_ref,
                 p0_ref, p1_ref):
    s_len = qt_ref.shape[3]
    tk, qb = s0_ref.shape
    s_refs = (s0_ref, s1_ref, s2_ref)
    p_refs = (p0_ref, p1_ref)

    units = []
    for d in range(s_len // tk):
        c0 = d * tk
        while c0 < s_len:
            c1 = min((c0 // qb + 1) * qb, s_len)
            units.append((d, c0, c1))
            c0 = c1

    def scores(u):
        d, c0, c1 = units[u]
        st = _dot(k_ref[0, 0, d * tk:(d + 1) * tk, :], qt_ref[0, 0, :, c0:c1])
        if c0 == d * tk:
            key = lax.broadcasted_iota(jnp.int32, (tk, tk), 0)
            qry = lax.broadcasted_iota(jnp.int32, (tk, tk), 1)
            tri = jnp.where(key <= qry, st[:, :tk], jnp.finfo(F32).min)
            st = tri if c1 - c0 == tk else jnp.concatenate([tri, st[:, tk:]], axis=1)
        s_refs[u % 3][:, :c1 - c0] = st
        return jnp.max(st, axis=0, keepdims=True)

    def softmax_step(u, st_max):
        d, c0, c1 = units[u]
        m_prev = m_ref[:, c0:c1]
        m_new = jnp.maximum(m_prev, st_max)
        alpha = jnp.exp2(m_prev - m_new)
        p = jnp.exp2(s_refs[u % 3][:, :c1 - c0] - m_new)
        p_refs[u % 2][:, :c1 - c0] = p.astype(BF16)
        m_ref[:, c0:c1] = m_new
        return alpha

    def accumulate(u, alpha):
        d, c0, c1 = units[u]
        pv = _dot(vt_ref[0, 0, :, d * tk:(d + 1) * tk], p_refs[u % 2][:, :c1 - c0])
        if d == 0:
            acc_ref[:, c0:c1] = pv
        else:
            acc_ref[:, c0:c1] = alpha * acc_ref[:, c0:c1] + pv

    m_ref[...] = jnp.full(m_ref.shape, -jnp.inf, F32)
    ahead = ATTN_SCORES_AHEAD
    maxes = {u: scores(u) for u in range(ahead)}
    alpha_prev = None
    for u in range(len(units)):
        if u + ahead < len(units):
            maxes[u + ahead] = scores(u + ahead)
        alpha = softmax_step(u, maxes.pop(u))
        if u:
            accumulate(u - 1, alpha_prev)
        alpha_prev = alpha
    accumulate(len(units) - 1, alpha_prev)
    o_ref[0, 0] = (acc_ref[0:V_HEAD, :] / acc_ref[V_HEAD:V_HEAD + 1, :]).T.astype(o_ref.dtype)


def _attention(qt, k, vt):
    b, nh, dk, s = qt.shape
    dv = vt.shape[2]
    assert dv == V_EXT
    tk = ATTN_KEY_CHUNK
    qb = ATTN_QUERY_BLOCK
    assert s % qb == 0 and qb % tk == 0
    return pl.pallas_call(
        _attn_kernel,
        grid=(b, nh),
        in_specs=[pl.BlockSpec((1, 1, dk, s), lambda i, h: (i, h, 0, 0)),
                  pl.BlockSpec((1, 1, s, dk), lambda i, h: (i, h, 0, 0)),
                  pl.BlockSpec((1, 1, dv, s), lambda i, h: (i, h, 0, 0))],
        out_specs=pl.BlockSpec((1, 1, s, V_HEAD), lambda i, h: (i, h, 0, 0)),
        out_shape=jax.ShapeDtypeStruct((b, nh, s, V_HEAD), BF16),
        scratch_shapes=[pltpu.VMEM((dv, s), F32), pltpu.VMEM((1, s), F32),
                        *[pltpu.VMEM((tk, qb), F32)] * 3, *[pltpu.VMEM((tk, qb), BF16)] * 2],
        compiler_params=_params(("parallel", "parallel")),
        name="causal_attention",
    )(qt, k, vt)


def _prep_wuq(w_uq):
    r = w_uq.shape[0]
    w = w_uq.reshape(r, N_HEADS, QK_HEAD)
    nope = w[:, :, :QK_NOPE].reshape(r, N_HEADS * QK_NOPE)
    rope = w[:, :, QK_NOPE:].reshape(r, N_HEADS * QK_ROPE)
    return jnp.concatenate([nope, rope], axis=-1).T.astype(BF16)


def kernel(x, c, positions, mod_w, mod_b, norm1_g, norm2_g, pool_w, pool_b, pool_scale, kv_in_g,
           w_dkv, ckv_norm_g, w_uk, w_uv, w_dq, q_norm_g, w_uq, w_o, w_up, conv_w, conv_b, w_down,
           final_g):
    b, s, d = x.shape
    depth = mod_w.shape[0]
    n_pool = pool_w.shape[0]
    kv_rank = ckv_norm_g.shape[0]

    cc, ss, cct, sst = _rope_tables(positions)
    mods = _mods(c, mod_w, mod_b).reshape(depth, b, N_MOD, d)
    row = lambda a: a.reshape(1, -1)

    k = vt = None
    for l in range(depth):
        ffn_w = (w_up[l].astype(BF16), conv_w[l], row(conv_b[l]), w_down[l].astype(BF16))
        final = row(final_g) if l == depth - 1 else None
        if l < n_pool:
            pool = (row(norm1_g[l]), pool_w[l].astype(BF16), row(pool_b[l]), row(pool_scale[l]))
            x = _layer_tail(x, mods[l], row(norm2_g[l]), *ffn_w, pool=pool, final_g=final)
        else:
            j = l - n_pool
            q = _q_proj(x, mods[l], row(norm1_g[l]), w_dq[j].astype(BF16), row(q_norm_g[j]),
                        _prep_wuq(w_uq[j]), cct, sst)
            o = _attention(q, k, vt)
            x = _layer_tail(x, mods[l], row(norm2_g[l]), *ffn_w, attn=(o, w_o[j].astype(BF16)),
                            final_g=final)
        if l == n_pool - 1:
            wdkv = jnp.concatenate([w_dkv, w_dkv[:, kv_rank:]], axis=1).astype(BF16)
            k, vt = _shared_kv(x, row(kv_in_g), wdkv, row(ckv_norm_g), w_uk.astype(BF16),
                               w_uv.T.astype(BF16), cc, ss)
    return x
```

```python
import functools
import math

import jax
import jax.numpy as jnp
from jax import lax
from jax.experimental import pallas as pl
from jax.experimental.pallas import tpu as pltpu

F32 = jnp.float32
BF16 = jnp.bfloat16

POOL_WINDOWS = (2, 4, 8, 16)
N_HEADS = 8
QK_NOPE = 128
QK_ROPE = 64
V_HEAD = 128
QK_HEAD = QK_NOPE + QK_ROPE
QK_PAD = 256
V_EXT = V_HEAD + 16
ROPE_THETA = 10000.0
CONV_WIDTH = 3
EPS = 1e-6
N_MOD = 6

LANES = 128
SUBLANES = 8
VMEM_LIMIT_BYTES = 56 * 1024 * 1024

SEQ_TILE = 512
LAYER_SUBTILES = 2
FF_CHUNK = 256
ATTN_KEY_CHUNK = 512
ATTN_QUERY_BLOCK = 256
ATTN_SCORES_AHEAD = 1
POOL_HALO = 16
MODS_N_TILE = 1536


def _params(sem, flags=None):
    return pltpu.CompilerParams(dimension_semantics=sem, vmem_limit_bytes=VMEM_LIMIT_BYTES,
                                flags=flags)


def _rmsnorm(x, g):
    ms = jnp.mean(x * x, axis=-1, keepdims=True)
    return x * lax.rsqrt(ms + EPS) * g


def _dot(a, b):
    return jnp.dot(a, b, preferred_element_type=F32)


def _rope_table_kernel(pos_ref, inv_ref, cc_ref, ss_ref, cct_ref, sst_ref):
    ang = inv_ref[...] * pos_ref[0]
    cos_t = jnp.cos(ang)
    sin_t = jnp.sin(ang)
    cct = jnp.concatenate([cos_t, cos_t], axis=0)
    sst = jnp.concatenate([-sin_t, sin_t], axis=0)
    cct_ref[0] = cct
    sst_ref[0] = sst
    cc_ref[0] = jnp.concatenate([cct, cct], axis=0).T
    ss_ref[0] = jnp.concatenate([sst, sst], axis=0).T


def _rope_tables(positions):
    b, s = positions.shape
    half = QK_ROPE // 2
    inv = 1.0 / (ROPE_THETA ** (jnp.arange(0, QK_ROPE, 2, dtype=F32) / QK_ROPE))
    pos = positions.astype(F32)[:, None, :]
    ts = SEQ_TILE
    tab = pl.BlockSpec((1, ts, LANES), lambda i, j: (i, j, 0))
    tab_t = pl.BlockSpec((1, QK_ROPE, ts), lambda i, j: (i, 0, j))
    return pl.pallas_call(
        _rope_table_kernel,
        grid=(b, s // ts),
        in_specs=[pl.BlockSpec((1, 1, ts), lambda i, j: (i, 0, j)),
                  pl.BlockSpec((half, 1), lambda i, j: (0, 0))],
        out_specs=[tab, tab, tab_t, tab_t],
        out_shape=[jax.ShapeDtypeStruct((b, s, LANES), F32)] * 2
                  + [jax.ShapeDtypeStruct((b, QK_ROPE, s), F32)] * 2,
        compiler_params=_params(("parallel", "parallel")),
        name="rope_tables",
    )(pos, inv[:, None])


def _mods_kernel(c_ref, w_ref, b_ref, o_ref):
    c = c_ref[...]
    sc = (c * jax.nn.sigmoid(c)).astype(BF16)
    o_ref[0] = _dot(sc, w_ref[0].astype(BF16)) + b_ref[0]


def _mods(c, mod_w, mod_b):
    depth, d, n = mod_w.shape
    b = c.shape[0]
    nt = MODS_N_TILE
    return pl.pallas_call(
        _mods_kernel,
        grid=(depth, n // nt),
        in_specs=[pl.BlockSpec((b, d), lambda l, j: (0, 0)),
                  pl.BlockSpec((1, d, nt), lambda l, j: (l, 0, j)),
                  pl.BlockSpec((1, 1, nt), lambda l, j: (l, 0, j))],
        out_specs=pl.BlockSpec((1, b, nt), lambda l, j: (l, 0, j)),
        out_shape=jax.ShapeDtypeStruct((depth, b, n), F32),
        compiler_params=_params(("parallel", "parallel")),
        name="adaln_mods",
    )(c, mod_w, mod_b.reshape(depth, 1, n))


def _shift_rows(a, k, prev_rows):
    rolled = pltpu.roll(a, k, axis=0)
    prev = pltpu.roll(prev_rows, k, axis=0)
    row = lax.broadcasted_iota(jnp.int32, prev_rows.shape, 0)
    head = jnp.where(row < k, prev, rolled[0:SUBLANES])
    return jnp.concatenate([head, rolled[SUBLANES:]], axis=0)


def _pooled_groups(h, halo, first_pos):
    rows, d = h.shape
    group = d // len(POOL_WINDOWS)
    t = first_pos + lax.broadcasted_iota(jnp.int32, (rows, 1), 0)
    pooled = []
    for gi, w in enumerate(POOL_WINDOWS):
        lo = gi * group
        hg = h[:, lo:lo + group]
        win = jnp.concatenate([halo[:, lo:lo + group], hg], axis=0)
        k = 1
        while k < w:
            win = win + pltpu.roll(win, k, axis=0)
            k *= 2
        cnt = jnp.minimum(t + 1, w).astype(F32)
        pooled.append((win[POOL_HALO:] / cnt - hg).astype(BF16))
    return pooled


def _layer_kernel(*refs, mixer, with_final):
    it = iter(refs)
    x_ref = next(it)
    mod_ref = next(it)
    if mixer == "attn":
        o_in_ref = next(it)
        wo_ref = next(it)
    else:
        g1_ref = next(it)
        pw_ref = next(it)
        pb_ref = next(it)
        ps_ref = next(it)
    g2_ref = next(it)
    wup_ref = next(it)
    cw_ref = next(it)
    cb_ref = next(it)
    wd_ref = next(it)
    if with_final:
        fg_ref = next(it)
    out_ref = next(it)
    carry_ref = next(it)
    gate_ref = next(it)
    if mixer == "pool":
        halo_ref = next(it)

    s = pl.program_id(1)
    ts = x_ref.shape[1]
    sub = ts // LAYER_SUBTILES
    f = wd_ref.shape[0]
    fc = FF_CHUNK
    n_chunks = f // fc

    @pl.when(s == 0)
    def _():
        carry_ref[...] = jnp.zeros(carry_ref.shape, F32)
        if mixer == "pool":
            halo_ref[...] = jnp.zeros(halo_ref.shape, F32)

    mod = mod_ref[0]

    halo = halo_ref[...] if mixer == "pool" else None

    def mixer_vector_part(t, anchor=None):
        nonlocal halo
        if mixer == "attn":
            return None
        x = x_ref[0, t * sub:(t + 1) * sub]
        if anchor is not None:
            x = x + anchor
        h1 = _rmsnorm(x, g1_ref[...]) * (1.0 + mod[1:2]) + mod[0:1]
        pooled = _pooled_groups(h1, halo, s * ts + t * sub)
        halo = h1[sub - POOL_HALO:sub]
        return pooled

    def mixer_matmul_part(t, pooled):
        rows = slice(t * sub, (t + 1) * sub)
        if mixer == "attn":
            o = jnp.concatenate([o_in_ref[0, hd, rows, :] for hd in range(N_HEADS)], axis=-1)
            y = _dot(o, wo_ref[...])
        else:
            y = jnp.concatenate([_dot(p, pw_ref[gi]) for gi, p in enumerate(pooled)], axis=-1)
            y = (y + pb_ref[...]) * ps_ref[...]
        x1 = x_ref[0, rows] + mod[2:3] * y
        out_ref[0, rows] = x1
        return (_rmsnorm(x1, g2_ref[...]) * (1.0 + mod[4:5]) + mod[3:4]).astype(BF16)

    def zero_after(value):
        bits = lax.bitcast_convert_type(value[0:1, 0:LANES], jnp.uint32)
        bits = lax.shift_right_logical(lax.shift_right_logical(bits, jnp.uint32(16)), jnp.uint32(16))
        zeros = lax.bitcast_convert_type(bits, F32)
        return jnp.concatenate([zeros] * (x_ref.shape[2] // LANES), axis=1)

    sqrt_half = math.sqrt(0.5)
    tails = [carry_ref[:, j * fc:(j + 1) * fc] for j in range(n_chunks)]
    h_next = mixer_matmul_part(0, mixer_vector_part(0))
    for t in range(LAYER_SUBTILES):
        rows = slice(t * sub, (t + 1) * sub)
        h = h_next
        up = lambda j: (_dot(h, wup_ref[:, j * fc:(j + 1) * fc]),
                        _dot(h, wup_ref[:, f + j * fc:f + (j + 1) * fc]))
        nxt = up(0)
        if t + 1 < LAYER_SUBTILES:
            pooled_next = mixer_vector_part(t + 1, anchor=zero_after(nxt[0]))
        for j in range(n_chunks):
            cols = slice(j * fc, (j + 1) * fc)
            a, v = nxt
            if j + 1 < n_chunks:
                nxt = up(j + 1)
            a1 = _shift_rows(a, 1, tails[j])
            a2 = _shift_rows(a, 2, tails[j])
            conv = (a2 * cw_ref[0:1, cols] + a1 * cw_ref[1:2, cols] + a * cw_ref[2:3, cols]
                    + cb_ref[:, cols])
            gate = 0.5 * conv * (1.0 + lax.erf(conv * sqrt_half))
            gate_ref[:, cols] = (gate * v).astype(BF16)
            tails[j] = a[sub - SUBLANES:sub]
        if t + 1 < LAYER_SUBTILES:
            h_next = mixer_matmul_part(t + 1, pooled_next)
        y = out_ref[0, rows] + mod[5:6] * _dot(gate_ref[...], wd_ref[...])
        if with_final:
            y = _rmsnorm(y, fg_ref[...])
        out_ref[0, rows] = y
    for j in range(n_chunks):
        carry_ref[:, j * fc:(j + 1) * fc] = tails[j]
    if mixer == "pool":
        halo_ref[...] = halo


def _layer_tail(x, mods_l, g2, wup, cw, cb, wd, attn=None, pool=None, final_g=None):
    b, s, d = x.shape
    ts = LAYER_SUBTILES * SEQ_TILE
    f = wd.shape[0]
    assert f % FF_CHUNK == 0 and s % ts == 0 and (attn is None) != (pool is None)
    const2 = lambda i, j: (0, 0)
    vec = pl.BlockSpec((1, d), const2)
    tile = pl.BlockSpec((1, ts, d), lambda i, j: (i, j, 0))
    single = dict(pipeline_mode=pl.Buffered(1))
    in_specs = [tile, pl.BlockSpec((1, N_MOD, d), lambda i, j: (i, 0, 0))]
    args = [x, mods_l]
    scratch = [pltpu.VMEM((SUBLANES, f), F32), pltpu.VMEM((SEQ_TILE, f), BF16)]
    if attn is not None:
        o, wo = attn
        in_specs += [pl.BlockSpec((1, o.shape[1], ts, o.shape[3]), lambda i, j: (i, 0, j, 0)),
                     pl.BlockSpec(wo.shape, const2, **single)]
        args += [o, wo]
    else:
        g1, pw, pb, ps = pool
        in_specs += [vec, pl.BlockSpec(pw.shape, lambda i, j: (0, 0, 0)), vec, vec]
        args += [g1, pw, pb, ps]
        scratch.append(pltpu.VMEM((POOL_HALO, d), F32))
    in_specs += [vec,
                 pl.BlockSpec(wup.shape, const2, **single),
                 pl.BlockSpec(cw.shape, const2),
                 pl.BlockSpec(cb.shape, const2),
                 pl.BlockSpec(wd.shape, const2, **single)]
    args += [g2, wup, cw, cb, wd]
    if final_g is not None:
        in_specs.append(vec)
        args.append(final_g)
    kern = functools.partial(_layer_kernel, mixer="attn" if attn is not None else "pool",
                             with_final=final_g is not None)
    return pl.pallas_call(
        kern,
        grid=(b, s // ts),
        in_specs=in_specs,
        out_specs=tile,
        out_shape=jax.ShapeDtypeStruct(x.shape, F32),
        scratch_shapes=scratch,
        compiler_params=_params(("arbitrary", "arbitrary")),
        name="layer_tail_" + ("attn" if attn is not None else "pool"),
    )(*args)


def _kv_kernel(x_ref, g_ref, wdkv_ref, cg_ref, wuk_ref, wuvt_ref, cc_ref, ss_ref, k_ref, vt_ref):
    ts = x_ref.shape[1]
    kv_rank = cg_ref.shape[1]
    xn = _rmsnorm(x_ref[0], g_ref[...]).astype(BF16)
    kv = _dot(xn, wdkv_ref[...])
    c_kv = _rmsnorm(kv[:, 0:kv_rank], cg_ref[...]).astype(BF16)
    kr = kv[:, kv_rank:kv_rank + LANES]
    k_rope = kr * cc_ref[0] + pltpu.roll(kr, QK_ROPE // 2, axis=1) * ss_ref[0]
    k_rope = k_rope[:, 0:QK_ROPE].astype(BF16)
    row = lax.broadcasted_iota(jnp.int32, (V_EXT - V_HEAD, ts), 0)
    ones_row = jnp.where(row == 0, 1.0, 0.0).astype(BF16)
    k_nope = _dot(c_kv, wuk_ref[...])
    vt = lax.dot_general(wuvt_ref[...], c_kv, (((1,), (1,)), ((), ())), preferred_element_type=F32)
    for hd in range(N_HEADS):
        k_ref[0, hd, :, 0:QK_NOPE] = k_nope[:, hd * QK_NOPE:(hd + 1) * QK_NOPE].astype(BF16)
        k_ref[0, hd, :, QK_NOPE:QK_HEAD] = k_rope
        k_ref[0, hd, :, QK_HEAD:QK_PAD] = jnp.zeros((ts, QK_PAD - QK_HEAD), BF16)
        vt_ref[0, hd, 0:V_HEAD, :] = vt[hd * V_HEAD:(hd + 1) * V_HEAD, :].astype(BF16)
        vt_ref[0, hd, V_HEAD:V_EXT, :] = ones_row


def _shared_kv(x, g, wdkv, cg, wuk, wuvt, cc, ss):
    b, s, d = x.shape
    ts = SEQ_TILE
    const2 = lambda i, j: (0, 0)
    full = lambda a: pl.BlockSpec(a.shape, const2)
    tab = pl.BlockSpec((1, ts, LANES), lambda i, j: (i, j, 0))
    return pl.pallas_call(
        _kv_kernel,
        grid=(b, s // ts),
        in_specs=[pl.BlockSpec((1, ts, d), lambda i, j: (i, j, 0)),
                  full(g), full(wdkv), full(cg), full(wuk), full(wuvt), tab, tab],
        out_specs=[pl.BlockSpec((1, N_HEADS, ts, QK_PAD), lambda i, j: (i, 0, j, 0)),
                   pl.BlockSpec((1, N_HEADS, V_EXT, ts), lambda i, j: (i, 0, 0, j))],
        out_shape=[jax.ShapeDtypeStruct((b, N_HEADS, s, QK_PAD), BF16),
                   jax.ShapeDtypeStruct((b, N_HEADS, V_EXT, s), BF16)],
        compiler_params=_params(("parallel", "parallel")),
        name="shared_kv",
    )(x, g, wdkv, cg, wuk, wuvt, cc, ss)


def _q_kernel(x_ref, mod_ref, g_ref, wdq_ref, qg_ref, wuqt_ref, cct_ref, sst_ref, qt_ref):
    mod = mod_ref[0]
    h = (_rmsnorm(x_ref[0], g_ref[...]) * (1.0 + mod[1:2]) + mod[0:1]).astype(BF16)
    c_q = _rmsnorm(_dot(h, wdq_ref[...]), qg_ref[...]).astype(BF16)
    scale = QK_HEAD ** -0.5 * math.log2(math.e)
    n_nope = N_HEADS * QK_NOPE
    n_rope = N_HEADS * QK_ROPE
    qt = lax.dot_general(wuqt_ref[...], c_q, (((1,), (1,)), ((), ())), preferred_element_type=F32)
    cct = jnp.concatenate([cct_ref[0]] * N_HEADS, axis=0)
    sst = jnp.concatenate([sst_ref[0]] * N_HEADS, axis=0)
    q_nope = qt[0:n_nope] * scale
    qr = qt[n_nope:n_nope + n_rope]
    half = QK_ROPE // 2
    qr_swapped = jnp.concatenate(
        [qr[hd * QK_ROPE + off:hd * QK_ROPE + off + half] for hd in range(N_HEADS) for off in (half, 0)],
        axis=0)
    q_rope = (qr * cct + qr_swapped * sst) * scale
    for hd in range(N_HEADS):
        qt_ref[0, hd, 0:QK_NOPE, :] = q_nope[hd * QK_NOPE:(hd + 1) * QK_NOPE].astype(BF16)
        qt_ref[0, hd, QK_NOPE:QK_HEAD, :] = q_rope[hd * QK_ROPE:(hd + 1) * QK_ROPE].astype(BF16)
        qt_ref[0, hd, QK_HEAD:QK_PAD, :] = jnp.zeros((QK_PAD - QK_HEAD, qt.shape[1]), BF16)


def _q_proj(x, mods_l, g, wdq, qg, wuqt, cct, sst):
    b, s, d = x.shape
    ts = SEQ_TILE
    const2 = lambda i, j: (0, 0)
    full = lambda a: pl.BlockSpec(a.shape, const2)
    tab_t = pl.BlockSpec((1, QK_ROPE, ts), lambda i, j: (i, 0, j))
    return pl.pallas_call(
        _q_kernel,
        grid=(b, s // ts),
        in_specs=[pl.BlockSpec((1, ts, d), lambda i, j: (i, j, 0)),
                  pl.BlockSpec((1, N_MOD, d), lambda i, j: (i, 0, 0)),
                  full(g), full(wdq), full(qg), full(wuqt), tab_t, tab_t],
        out_specs=pl.BlockSpec((1, N_HEADS, QK_PAD, ts), lambda i, j: (i, 0, 0, j)),
        out_shape=jax.ShapeDtypeStruct((b, N_HEADS, QK_PAD, s), BF16),
        compiler_params=_params(("parallel", "parallel")),
        name="q_proj",
    )(x, mods_l, g, wdq, qg, wuqt, cct, sst)


def _attn_kernel(qt_ref, k_ref, vt_ref, o_ref, acc_ref, m_ref, s0_ref, s1_ref, s2_ref,
                 p0_ref, p1_ref):
    s_len = qt_ref.shape[3]
    tk, qb = s0_ref.shape
    s_refs = (s0_ref, s1_ref, s2_ref)
    p_refs = (p0_ref, p1_ref)

    units = []
    for d in range(s_len // tk):
        c0 = d * tk
        while c0 < s_len:
            c1 = min((c0 // qb + 1) * qb, s_len)
            units.append((d, c0, c1))
            c0 = c1

    def scores(u):
        d, c0, c1 = units[u]
        st = _dot(k_ref[0, 0, d * tk:(d + 1) * tk, :], qt_ref[0, 0, :, c0:c1])
        w_diag = min(c1, (d + 1) * tk) - c0
        if w_diag > 0:
            key = lax.broadcasted_iota(jnp.int32, (tk, w_diag), 0) + d * tk
            qry = lax.broadcasted_iota(jnp.int32, (tk, w_diag), 1) + c0
            tri = jnp.where(key <= qry, st[:, :w_diag], jnp.finfo(F32).min)
            st = tri if w_diag == c1 - c0 else jnp.concatenate([tri, st[:, w_diag:]], axis=1)
        s_refs[u % 3][:, :c1 - c0] = st
        return jnp.max(st, axis=0, keepdims=True)

    def softmax_step(u, st_max):
        d, c0, c1 = units[u]
        m_prev = m_ref[:, c0:c1]
        m_new = jnp.maximum(m_prev, st_max)
        alpha = jnp.exp2(m_prev - m_new)
        p = jnp.exp2(s_refs[u % 3][:, :c1 - c0] - m_new)
        p_refs[u % 2][:, :c1 - c0] = p.astype(BF16)
        m_ref[:, c0:c1] = m_new
        return alpha

    def accumulate(u, alpha):
        d, c0, c1 = units[u]
        pv = _dot(vt_ref[0, 0, :, d * tk:(d + 1) * tk], p_refs[u % 2][:, :c1 - c0])
        if d == 0:
            acc_ref[:, c0:c1] = pv
        else:
            acc_ref[:, c0:c1] = alpha * acc_ref[:, c0:c1] + pv

    m_ref[...] = jnp.full(m_ref.shape, -jnp.inf, F32)
    ahead = ATTN_SCORES_AHEAD
    maxes = {u: scores(u) for u in range(ahead)}
    alpha_prev = None
    for u in range(len(units)):
        if u + ahead < len(units):
            maxes[u + ahead] = scores(u + ahead)
        alpha = softmax_step(u, maxes.pop(u))
        if u:
            accumulate(u - 1, alpha_prev)
        alpha_prev = alpha
    accumulate(len(units) - 1, alpha_prev)
    o_ref[0, 0] = (acc_ref[0:V_HEAD, :] / acc_ref[V_HEAD:V_HEAD + 1, :]).T.astype(o_ref.dtype)


def _attention(qt, k, vt):
    b, nh, dk, s = qt.shape
    dv = vt.shape[2]
    assert dv == V_EXT
    tk = ATTN_KEY_CHUNK
    qb = ATTN_QUERY_BLOCK
    assert s % qb == 0 and s % tk == 0
    return pl.pallas_call(
        _attn_kernel,
        grid=(b, nh),
        in_specs=[pl.BlockSpec((1, 1, dk, s), lambda i, h: (i, h, 0, 0)),
                  pl.BlockSpec((1, 1, s, dk), lambda i, h: (i, h, 0, 0)),
                  pl.BlockSpec((1, 1, dv, s), lambda i, h: (i, h, 0, 0))],
        out_specs=pl.BlockSpec((1, 1, s, V_HEAD), lambda i, h: (i, h, 0, 0)),
        out_shape=jax.ShapeDtypeStruct((b, nh, s, V_HEAD), BF16),
        scratch_shapes=[pltpu.VMEM((dv, s), F32), pltpu.VMEM((1, s), F32),
                        *[pltpu.VMEM((tk, qb), F32)] * 3, *[pltpu.VMEM((tk, qb), BF16)] * 2],
        compiler_params=_params(("parallel", "parallel")),
        name="causal_attention",
    )(qt, k, vt)


def _prep_wuq(w_uq):
    r = w_uq.shape[0]
    w = w_uq.reshape(r, N_HEADS, QK_HEAD)
    nope = w[:, :, :QK_NOPE].reshape(r, N_HEADS * QK_NOPE)
    rope = w[:, :, QK_NOPE:].reshape(r, N_HEADS * QK_ROPE)
    return jnp.concatenate([nope, rope], axis=-1).T.astype(BF16)


def kernel(x, c, positions, mod_w, mod_b, norm1_g, norm2_g, pool_w, pool_b, pool_scale, kv_in_g,
           w_dkv, ckv_norm_g, w_uk, w_uv, w_dq, q_norm_g, w_uq, w_o, w_up, conv_w, conv_b, w_down,
           final_g):
    b, s, d = x.shape
    depth = mod_w.shape[0]
    n_pool = pool_w.shape[0]
    kv_rank = ckv_norm_g.shape[0]

    cc, ss, cct, sst = _rope_tables(positions)
    mods = _mods(c, mod_w, mod_b).reshape(depth, b, N_MOD, d)
    row = lambda a: a.reshape(1, -1)

    k = vt = None
    for l in range(depth):
        ffn_w = (w_up[l].astype(BF16), conv_w[l], row(conv_b[l]), w_down[l].astype(BF16))
        final = row(final_g) if l == depth - 1 else None
        if l < n_pool:
            pool = (row(norm1_g[l]), pool_w[l].astype(BF16), row(pool_b[l]), row(pool_scale[l]))
            x = _layer_tail(x, mods[l], row(norm2_g[l]), *ffn_w, pool=pool, final_g=final)
        else:
            j = l - n_pool
            q = _q_proj(x, mods[l], row(norm1_g[l]), w_dq[j].astype(BF16), row(q_norm_g[j]),
                        _prep_wuq(w_uq[j]), cct, sst)
            o = _attention(q, k, vt)
            x = _layer_tail(x, mods[l], row(norm2_g[l]), *ffn_w, attn=(o, w_o[j].astype(BF16)),
                            final_g=final)
        if l == n_pool - 1:
            wdkv = jnp.concatenate([w_dkv, w_dkv[:, kv_rank:]], axis=1).astype(BF16)
            k, vt = _shared_kv(x, row(kv_in_g), wdkv, row(ckv_norm_g), w_uk.astype(BF16),
                               w_uv.T.astype(BF16), cc, ss)
    return x
```

```python
import functools
import math

import jax
import jax.numpy as jnp
from jax import lax
from jax.experimental import pallas as pl
from jax.experimental.pallas import tpu as pltpu

F32 = jnp.float32
BF16 = jnp.bfloat16

POOL_WINDOWS = (2, 4, 8, 16)
N_HEADS = 8
QK_NOPE = 128
QK_ROPE = 64
V_HEAD = 128
QK_HEAD = QK_NOPE + QK_ROPE
QK_PAD = 256
V_EXT = V_HEAD + 16
ROPE_THETA = 10000.0
CONV_WIDTH = 3
EPS = 1e-6
N_MOD = 6

LANES = 128
SUBLANES = 8
VMEM_LIMIT_BYTES = 56 * 1024 * 1024

SEQ_TILE = 512
LAYER_SUBTILES = 2
PROJ_TILE = 1024
FF_CHUNK = 256
ATTN_KEY_CHUNK = 256
ATTN_QUERY_BLOCK = 512
ATTN_SCORES_AHEAD = 1
POOL_HALO = 16
MODS_N_TILE = 1536


def _params(sem, flags=None):
    return pltpu.CompilerParams(dimension_semantics=sem, vmem_limit_bytes=VMEM_LIMIT_BYTES,
                                flags=flags)


def _rmsnorm(x, g):
    ms = jnp.mean(x * x, axis=-1, keepdims=True)
    return x * lax.rsqrt(ms + EPS) * g


def _dot(a, b):
    return jnp.dot(a, b, preferred_element_type=F32)


def _rope_table_kernel(pos_ref, inv_ref, cc_ref, ss_ref, cct_ref, sst_ref):
    ang = inv_ref[...] * pos_ref[0]
    cos_t = jnp.cos(ang)
    sin_t = jnp.sin(ang)
    cct = jnp.concatenate([cos_t, cos_t], axis=0)
    sst = jnp.concatenate([-sin_t, sin_t], axis=0)
    cct_ref[0] = cct
    sst_ref[0] = sst
    cc_ref[0] = jnp.concatenate([cct, cct], axis=0).T
    ss_ref[0] = jnp.concatenate([sst, sst], axis=0).T


def _rope_tables(positions):
    b, s = positions.shape
    half = QK_ROPE // 2
    inv = 1.0 / (ROPE_THETA ** (jnp.arange(0, QK_ROPE, 2, dtype=F32) / QK_ROPE))
    pos = positions.astype(F32)[:, None, :]
    ts = SEQ_TILE
    tab = pl.BlockSpec((1, ts, LANES), lambda i, j: (i, j, 0))
    tab_t = pl.BlockSpec((1, QK_ROPE, ts), lambda i, j: (i, 0, j))
    return pl.pallas_call(
        _rope_table_kernel,
        grid=(b, s // ts),
        in_specs=[pl.BlockSpec((1, 1, ts), lambda i, j: (i, 0, j)),
                  pl.BlockSpec((half, 1), lambda i, j: (0, 0))],
        out_specs=[tab, tab, tab_t, tab_t],
        out_shape=[jax.ShapeDtypeStruct((b, s, LANES), F32)] * 2
                  + [jax.ShapeDtypeStruct((b, QK_ROPE, s), F32)] * 2,
        compiler_params=_params(("parallel", "parallel")),
        name="rope_tables",
    )(pos, inv[:, None])


def _mods_kernel(c_ref, w_ref, b_ref, o_ref):
    c = c_ref[...]
    sc = (c * jax.nn.sigmoid(c)).astype(BF16)
    o_ref[0] = _dot(sc, w_ref[0].astype(BF16)) + b_ref[0]


def _mods(c, mod_w, mod_b):
    depth, d, n = mod_w.shape
    b = c.shape[0]
    nt = MODS_N_TILE
    return pl.pallas_call(
        _mods_kernel,
        grid=(depth, n // nt),
        in_specs=[pl.BlockSpec((b, d), lambda l, j: (0, 0)),
                  pl.BlockSpec((1, d, nt), lambda l, j: (l, 0, j)),
                  pl.BlockSpec((1, 1, nt), lambda l, j: (l, 0, j))],
        out_specs=pl.BlockSpec((1, b, nt), lambda l, j: (l, 0, j)),
        out_shape=jax.ShapeDtypeStruct((depth, b, n), F32),
        compiler_params=_params(("parallel", "parallel")),
        name="adaln_mods",
    )(c, mod_w, mod_b.reshape(depth, 1, n))


def _shift_rows(a, k, prev_rows):
    rolled = pltpu.roll(a, k, axis=0)
    prev = pltpu.roll(prev_rows, k, axis=0)
    row = lax.broadcasted_iota(jnp.int32, prev_rows.shape, 0)
    head = jnp.where(row < k, prev, rolled[0:SUBLANES])
    return jnp.concatenate([head, rolled[SUBLANES:]], axis=0)


def _pooled_groups(h, halo, first_pos):
    rows, d = h.shape
    group = d // len(POOL_WINDOWS)
    t = first_pos + lax.broadcasted_iota(jnp.int32, (rows, 1), 0)
    pooled = []
    for gi, w in enumerate(POOL_WINDOWS):
        lo = gi * group
        hg = h[:, lo:lo + group]
        win = jnp.concatenate([halo[:, lo:lo + group], hg], axis=0)
        k = 1
        while k < w:
            win = win + pltpu.roll(win, k, axis=0)
            k *= 2
        cnt = jnp.minimum(t + 1, w).astype(F32)
        pooled.append((win[POOL_HALO:] / cnt - hg).astype(BF16))
    return pooled


def _layer_kernel(*refs, mixer, with_final):
    it = iter(refs)
    x_ref = next(it)
    mod_ref = next(it)
    if mixer == "attn":
        o_in_ref = next(it)
        wo_ref = next(it)
    else:
        g1_ref = next(it)
        pw_ref = next(it)
        pb_ref = next(it)
        ps_ref = next(it)
    g2_ref = next(it)
    wup_ref = next(it)
    cw_ref = next(it)
    cb_ref = next(it)
    wd_ref = next(it)
    if with_final:
        fg_ref = next(it)
    out_ref = next(it)
    carry_ref = next(it)
    gate_ref = next(it)
    if mixer == "pool":
        halo_ref = next(it)

    s = pl.program_id(1)
    ts = x_ref.shape[1]
    sub = ts // LAYER_SUBTILES
    f = wd_ref.shape[0]
    fc = FF_CHUNK
    n_chunks = f // fc

    @pl.when(s == 0)
    def _():
        carry_ref[...] = jnp.zeros(carry_ref.shape, F32)
        if mixer == "pool":
            halo_ref[...] = jnp.zeros(halo_ref.shape, F32)

    mod = mod_ref[0]

    halo = halo_ref[...] if mixer == "pool" else None

    def mixer_vector_part(t, anchor=None):
        nonlocal halo
        if mixer == "attn":
            return None
        x = x_ref[0, t * sub:(t + 1) * sub]
        if anchor is not None:
            x = x + anchor
        h1 = _rmsnorm(x, g1_ref[...]) * (1.0 + mod[1:2]) + mod[0:1]
        pooled = _pooled_groups(h1, halo, s * ts + t * sub)
        halo = h1[sub - POOL_HALO:sub]
        return pooled

    def mixer_matmul_part(t, pooled):
        rows = slice(t * sub, (t + 1) * sub)
        if mixer == "attn":
            o = jnp.concatenate([o_in_ref[0, hd, rows, :] for hd in range(N_HEADS)], axis=-1)
            y = _dot(o, wo_ref[...])
        else:
            y = jnp.concatenate([_dot(p, pw_ref[gi]) for gi, p in enumerate(pooled)], axis=-1)
            y = (y + pb_ref[...]) * ps_ref[...]
        x1 = x_ref[0, rows] + mod[2:3] * y
        out_ref[0, rows] = x1
        return (_rmsnorm(x1, g2_ref[...]) * (1.0 + mod[4:5]) + mod[3:4]).astype(BF16)

    def zero_after(value):
        bits = lax.bitcast_convert_type(value[0:1, 0:LANES], jnp.uint32)
        bits = lax.shift_right_logical(lax.shift_right_logical(bits, jnp.uint32(16)), jnp.uint32(16))
        zeros = lax.bitcast_convert_type(bits, F32)
        return jnp.concatenate([zeros] * (x_ref.shape[2] // LANES), axis=1)

    sqrt_half = math.sqrt(0.5)
    tails = [carry_ref[:, j * fc:(j + 1) * fc] for j in range(n_chunks)]
    h_next = mixer_matmul_part(0, mixer_vector_part(0))
    for t in range(LAYER_SUBTILES):
        rows = slice(t * sub, (t + 1) * sub)
        h = h_next
        up = lambda j: (_dot(h, wup_ref[:, j * fc:(j + 1) * fc]),
                        _dot(h, wup_ref[:, f + j * fc:f + (j + 1) * fc]))
        nxt = up(0)
        if t + 1 < LAYER_SUBTILES:
            pooled_next = mixer_vector_part(t + 1, anchor=zero_after(nxt[0]))
        for j in range(n_chunks):
            cols = slice(j * fc, (j + 1) * fc)
            a, v = nxt
            if j + 1 < n_chunks:
                nxt = up(j + 1)
            a1 = _shift_rows(a, 1, tails[j])
            a2 = _shift_rows(a, 2, tails[j])
            conv = (a2 * cw_ref[0:1, cols] + a1 * cw_ref[1:2, cols] + a * cw_ref[2:3, cols]
                    + cb_ref[:, cols])
            gate = 0.5 * conv * (1.0 + lax.erf(conv * sqrt_half))
            gate_ref[:, cols] = (gate * v).astype(BF16)
            tails[j] = a[sub - SUBLANES:sub]
        if t + 1 < LAYER_SUBTILES:
            h_next = mixer_matmul_part(t + 1, pooled_next)
        y = out_ref[0, rows] + mod[5:6] * _dot(gate_ref[...], wd_ref[...])
        if with_final:
            y = _rmsnorm(y, fg_ref[...])
        out_ref[0, rows] = y
    for j in range(n_chunks):
        carry_ref[:, j * fc:(j + 1) * fc] = tails[j]
    if mixer == "pool":
        halo_ref[...] = halo


def _layer_tail(x, mods_l, g2, wup_all, cw, cb, wd_all, layer, attn=None, pool=None, final_g=None):
    b, s, d = x.shape
    ts = LAYER_SUBTILES * SEQ_TILE
    f = wd_all.shape[1]
    assert f % FF_CHUNK == 0 and s % ts == 0 and (attn is None) != (pool is None)
    layer_block = lambda i, j: (layer, 0, 0)
    const2 = lambda i, j: (0, 0)
    vec = pl.BlockSpec((1, d), const2)
    tile = pl.BlockSpec((1, ts, d), lambda i, j: (i, j, 0))
    single = dict(pipeline_mode=pl.Buffered(1))
    in_specs = [tile, pl.BlockSpec((1, N_MOD, d), lambda i, j: (i, 0, 0))]
    args = [x, mods_l]
    scratch = [pltpu.VMEM((SUBLANES, f), F32), pltpu.VMEM((SEQ_TILE, f), BF16)]
    if attn is not None:
        o, wo = attn
        in_specs += [pl.BlockSpec((1, o.shape[1], ts, o.shape[3]), lambda i, j: (i, 0, j, 0)),
                     pl.BlockSpec(wo.shape, const2, **single)]
        args += [o, wo]
    else:
        g1, pw, pb, ps = pool
        in_specs += [vec, pl.BlockSpec(pw.shape, lambda i, j: (0, 0, 0)), vec, vec]
        args += [g1, pw, pb, ps]
        scratch.append(pltpu.VMEM((POOL_HALO, d), F32))
    in_specs += [vec,
                 pl.BlockSpec((None,) + wup_all.shape[1:], layer_block, **single),
                 pl.BlockSpec(cw.shape, const2),
                 pl.BlockSpec(cb.shape, const2),
                 pl.BlockSpec((None,) + wd_all.shape[1:], layer_block, **single)]
    args += [g2, wup_all, cw, cb, wd_all]
    if final_g is not None:
        in_specs.append(vec)
        args.append(final_g)
    kern = functools.partial(_layer_kernel, mixer="attn" if attn is not None else "pool",
                             with_final=final_g is not None)
    return pl.pallas_call(
        kern,
        grid=(b, s // ts),
        in_specs=in_specs,
        out_specs=tile,
        out_shape=jax.ShapeDtypeStruct(x.shape, F32),
        scratch_shapes=scratch,
        compiler_params=_params(("arbitrary", "arbitrary")),
        name="layer_tail_" + ("attn" if attn is not None else "pool"),
    )(*args)


def _kv_kernel(x_ref, g_ref, wdkv_ref, cg_ref, wuk_ref, wuvt_ref, cc_ref, ss_ref, k_ref, vt_ref):
    ts = x_ref.shape[1]
    kv_rank = cg_ref.shape[1]
    xn = _rmsnorm(x_ref[0], g_ref[...]).astype(BF16)
    kv = _dot(xn, wdkv_ref[...])
    c_kv = _rmsnorm(kv[:, 0:kv_rank], cg_ref[...]).astype(BF16)
    kr = kv[:, kv_rank:kv_rank + LANES]
    k_rope = kr * cc_ref[0] + pltpu.roll(kr, QK_ROPE // 2, axis=1) * ss_ref[0]
    k_rope = k_rope[:, 0:QK_ROPE].astype(BF16)
    row = lax.broadcasted_iota(jnp.int32, (V_EXT - V_HEAD, ts), 0)
    ones_row = jnp.where(row == 0, 1.0, 0.0).astype(BF16)
    k_nope = _dot(c_kv, wuk_ref[...])
    vt = lax.dot_general(wuvt_ref[...], c_kv, (((1,), (1,)), ((), ())), preferred_element_type=F32)
    for hd in range(N_HEADS):
        k_ref[0, hd, :, 0:QK_NOPE] = k_nope[:, hd * QK_NOPE:(hd + 1) * QK_NOPE].astype(BF16)
        k_ref[0, hd, :, QK_NOPE:QK_HEAD] = k_rope
        k_ref[0, hd, :, QK_HEAD:QK_PAD] = jnp.zeros((ts, QK_PAD - QK_HEAD), BF16)
        vt_ref[0, hd, 0:V_HEAD, :] = vt[hd * V_HEAD:(hd + 1) * V_HEAD, :].astype(BF16)
        vt_ref[0, hd, V_HEAD:V_EXT, :] = ones_row


def _shared_kv(x, g, wdkv, cg, wuk, wuvt, cc, ss):
    b, s, d = x.shape
    ts = PROJ_TILE
    const2 = lambda i, j: (0, 0)
    full = lambda a: pl.BlockSpec(a.shape, const2)
    tab = pl.BlockSpec((1, ts, LANES), lambda i, j: (i, j, 0))
    return pl.pallas_call(
        _kv_kernel,
        grid=(b, s // ts),
        in_specs=[pl.BlockSpec((1, ts, d), lambda i, j: (i, j, 0)),
                  full(g), full(wdkv), full(cg), full(wuk), full(wuvt), tab, tab],
        out_specs=[pl.BlockSpec((1, N_HEADS, ts, QK_PAD), lambda i, j: (i, 0, j, 0)),
                   pl.BlockSpec((1, N_HEADS, V_EXT, ts), lambda i, j: (i, 0, 0, j))],
        out_shape=[jax.ShapeDtypeStruct((b, N_HEADS, s, QK_PAD), BF16),
                   jax.ShapeDtypeStruct((b, N_HEADS, V_EXT, s), BF16)],
        compiler_params=_params(("parallel", "parallel")),
        name="shared_kv",
    )(x, g, wdkv, cg, wuk, wuvt, cc, ss)


def _q_kernel(x_ref, mod_ref, g_ref, wdq_ref, qg_ref, wuqt_ref, cct_ref, sst_ref, qt_ref):
    mod = mod_ref[0]
    h = (_rmsnorm(x_ref[0], g_ref[...]) * (1.0 + mod[1:2]) + mod[0:1]).astype(BF16)
    c_q = _rmsnorm(_dot(h, wdq_ref[...]), qg_ref[...]).astype(BF16)
    scale = QK_HEAD ** -0.5 * math.log2(math.e)
    n_nope = N_HEADS * QK_NOPE
    n_rope = N_HEADS * QK_ROPE
    qt = lax.dot_general(wuqt_ref[...], c_q, (((1,), (1,)), ((), ())), preferred_element_type=F32)
    cct = jnp.concatenate([cct_ref[0]] * N_HEADS, axis=0)
    sst = jnp.concatenate([sst_ref[0]] * N_HEADS, axis=0)
    q_nope = qt[0:n_nope] * scale
    qr = qt[n_nope:n_nope + n_rope]
    half = QK_ROPE // 2
    qr_swapped = jnp.concatenate(
        [qr[hd * QK_ROPE + off:hd * QK_ROPE + off + half] for hd in range(N_HEADS) for off in (half, 0)],
        axis=0)
    q_rope = (qr * cct + qr_swapped * sst) * scale
    for hd in range(N_HEADS):
        qt_ref[0, hd, 0:QK_NOPE, :] = q_nope[hd * QK_NOPE:(hd + 1) * QK_NOPE].astype(BF16)
        qt_ref[0, hd, QK_NOPE:QK_HEAD, :] = q_rope[hd * QK_ROPE:(hd + 1) * QK_ROPE].astype(BF16)
        qt_ref[0, hd, QK_HEAD:QK_PAD, :] = jnp.zeros((QK_PAD - QK_HEAD, qt.shape[1]), BF16)


def _q_proj(x, mods_l, g, wdq, qg, wuqt, cct, sst):
    b, s, d = x.shape
    ts = PROJ_TILE
    const2 = lambda i, j: (0, 0)
    full = lambda a: pl.BlockSpec(a.shape, const2)
    tab_t = pl.BlockSpec((1, QK_ROPE, ts), lambda i, j: (i, 0, j))
    return pl.pallas_call(
        _q_kernel,
        grid=(b, s // ts),
        in_specs=[pl.BlockSpec((1, ts, d), lambda i, j: (i, j, 0)),
                  pl.BlockSpec((1, N_MOD, d), lambda i, j: (i, 0, 0)),
                  full(g), full(wdq), full(qg), full(wuqt), tab_t, tab_t],
        out_specs=pl.BlockSpec((1, N_HEADS, QK_PAD, ts), lambda i, j: (i, 0, 0, j)),
        out_shape=jax.ShapeDtypeStruct((b, N_HEADS, QK_PAD, s), BF16),
        compiler_params=_params(("parallel", "parallel")),
        name="q_proj",
    )(x, mods_l, g, wdq, qg, wuqt, cct, sst)


def _attn_kernel(qt_ref, k_ref, vt_ref, o_ref, acc_ref, m_ref, s0_ref, s1_ref, s2_ref,
                 p0_ref, p1_ref):
    s_len = qt_ref.shape[3]
    tk, qb = s0_ref.shape
    s_refs = (s0_ref, s1_ref, s2_ref)
    p_refs = (p0_ref, p1_ref)

    units = []
    for d in range(s_len // tk):
        c0 = d * tk
        while c0 < s_len:
            c1 = min((c0 // qb + 1) * qb, s_len)
            units.append((d, c0, c1))
            c0 = c1

    def scores(u):
        d, c0, c1 = units[u]
        st = _dot(k_ref[0, 0, d * tk:(d + 1) * tk, :], qt_ref[0, 0, :, c0:c1])
        w_diag = min(c1, (d + 1) * tk) - c0
        if w_diag > 0:
            key = lax.broadcasted_iota(jnp.int32, (tk, w_diag), 0) + d * tk
            qry = lax.broadcasted_iota(jnp.int32, (tk, w_diag), 1) + c0
            tri = jnp.where(key <= qry, st[:, :w_diag], jnp.finfo(F32).min)
            st = tri if w_diag == c1 - c0 else jnp.concatenate([tri, st[:, w_diag:]], axis=1)
        s_refs[u % 3][:, :c1 - c0] = st
        return jnp.max(st, axis=0, keepdims=True)

    def softmax_step(u, st_max):
        d, c0, c1 = units[u]
        m_prev = m_ref[:, c0:c1]
        m_new = jnp.maximum(m_prev, st_max)
        alpha = jnp.exp2(m_prev - m_new)
        p = jnp.exp2(s_refs[u % 3][:, :c1 - c0] - m_new)
        p_refs[u % 2][:, :c1 - c0] = p.astype(BF16)
        m_ref[:, c0:c1] = m_new
        return alpha

    def accumulate(u, alpha):
        d, c0, c1 = units[u]
        pv = _dot(vt_ref[0, 0, :, d * tk:(d + 1) * tk], p_refs[u % 2][:, :c1 - c0])
        if d == 0:
            acc_ref[:, c0:c1] = pv
        else:
            acc_ref[:, c0:c1] = alpha * acc_ref[:, c0:c1] + pv

    m_ref[...] = jnp.full(m_ref.shape, -jnp.inf, F32)
    ahead = ATTN_SCORES_AHEAD
    maxes = {u: scores(u) for u in range(ahead)}
    alpha_prev = None
    for u in range(len(units)):
        if u + ahead < len(units):
            maxes[u + ahead] = scores(u + ahead)
        alpha = softmax_step(u, maxes.pop(u))
        if u:
            accumulate(u - 1, alpha_prev)
        alpha_prev = alpha
    accumulate(len(units) - 1, alpha_prev)
    o_ref[0, 0] = (acc_ref[0:V_HEAD, :] / acc_ref[V_HEAD:V_HEAD + 1, :]).T.astype(o_ref.dtype)


def _attention(qt, k, vt):
    b, nh, dk, s = qt.shape
    dv = vt.shape[2]
    assert dv == V_EXT
    tk = ATTN_KEY_CHUNK
    qb = ATTN_QUERY_BLOCK
    assert s % qb == 0 and s % tk == 0
    return pl.pallas_call(
        _attn_kernel,
        grid=(b, nh),
        in_specs=[pl.BlockSpec((1, 1, dk, s), lambda i, h: (i, h, 0, 0)),
                  pl.BlockSpec((1, 1, s, dk), lambda i, h: (i, h, 0, 0)),
                  pl.BlockSpec((1, 1, dv, s), lambda i, h: (i, h, 0, 0))],
        out_specs=pl.BlockSpec((1, 1, s, V_HEAD), lambda i, h: (i, h, 0, 0)),
        out_shape=jax.ShapeDtypeStruct((b, nh, s, V_HEAD), BF16),
        scratch_shapes=[pltpu.VMEM((dv, s), F32), pltpu.VMEM((1, s), F32),
                        *[pltpu.VMEM((tk, qb), F32)] * 3, *[pltpu.VMEM((tk, qb), BF16)] * 2],
        compiler_params=_params(("parallel", "parallel")),
        name="causal_attention",
    )(qt, k, vt)


def _prep_wuq(w_uq):
    r = w_uq.shape[0]
    w = w_uq.reshape(r, N_HEADS, QK_HEAD)
    nope = w[:, :, :QK_NOPE].reshape(r, N_HEADS * QK_NOPE)
    rope = w[:, :, QK_NOPE:].reshape(r, N_HEADS * QK_ROPE)
    return jnp.concatenate([nope, rope], axis=-1).T.astype(BF16)


def kernel(x, c, positions, mod_w, mod_b, norm1_g, norm2_g, pool_w, pool_b, pool_scale, kv_in_g,
           w_dkv, ckv_norm_g, w_uk, w_uv, w_dq, q_norm_g, w_uq, w_o, w_up, conv_w, conv_b, w_down,
           final_g):
    b, s, d = x.shape
    depth = mod_w.shape[0]
    n_pool = pool_w.shape[0]
    kv_rank = ckv_norm_g.shape[0]

    cc, ss, cct, sst = _rope_tables(positions)
    mods = _mods(c, mod_w, mod_b).reshape(depth, b, N_MOD, d)
    row = lambda a: a.reshape(1, -1)

    w_up_b = w_up.astype(BF16)
    w_down_b = w_down.astype(BF16)
    k = vt = None
    for l in range(depth):
        ffn_w = (w_up_b, conv_w[l], row(conv_b[l]), w_down_b, l)
        final = row(final_g) if l == depth - 1 else None
        if l < n_pool:
            pool = (row(norm1_g[l]), pool_w[l].astype(BF16), row(pool_b[l]), row(pool_scale[l]))
            x = _layer_tail(x, mods[l], row(norm2_g[l]), *ffn_w, pool=pool, final_g=final)
        else:
            j = l - n_pool
            q = _q_proj(x, mods[l], row(norm1_g[l]), w_dq[j].astype(BF16), row(q_norm_g[j]),
                        _prep_wuq(w_uq[j]), cct, sst)
            o = _attention(q, k, vt)
            x = _layer_tail(x, mods[l], row(norm2_g[l]), *ffn_w, attn=(o, w_o[j].astype(BF16)),
                            final_g=final)
        if l == n_pool - 1:
            wdkv = jnp.concatenate([w_dkv, w_dkv[:, kv_rank:]], axis=1).astype(BF16)
            k, vt = _shared_kv(x, row(kv_in_g), wdkv, row(ckv_norm_g), w_uk.astype(BF16),
                               w_uv.T.astype(BF16), cc, ss)
    return x
```

```python
import functools
import math

import jax
import jax.numpy as jnp
from jax import lax
from jax.experimental import pallas as pl
from jax.experimental.pallas import tpu as pltpu

F32 = jnp.float32
BF16 = jnp.bfloat16

POOL_WINDOWS = (2, 4, 8, 16)
N_HEADS = 8
QK_NOPE = 128
QK_ROPE = 64
V_HEAD = 128
QK_HEAD = QK_NOPE + QK_ROPE
QK_PAD = 256
V_EXT = V_HEAD + 16
ROPE_THETA = 10000.0
CONV_WIDTH = 3
EPS = 1e-6
N_MOD = 6

LANES = 128
SUBLANES = 8
VMEM_LIMIT_BYTES = 56 * 1024 * 1024

SEQ_TILE = 512
LAYER_SUBTILES = 2
PROJ_TILE = 1024
ROPE_TILE = 2048
FF_CHUNK = 256
ATTN_KEY_CHUNK = 256
ATTN_QUERY_BLOCK = 512
ATTN_SCORES_AHEAD = 1
POOL_HALO = 16
MODS_N_TILE = 1536


def _params(sem):
    return pltpu.CompilerParams(dimension_semantics=sem, vmem_limit_bytes=VMEM_LIMIT_BYTES)


def _rmsnorm(x, g):
    ms = jnp.mean(x * x, axis=-1, keepdims=True)
    return x * lax.rsqrt(ms + EPS) * g


def _dot(a, b):
    return jnp.dot(a, b, preferred_element_type=F32)


def _rope_table_kernel(pos_ref, inv_ref, cc_ref, ss_ref, cct_ref, sst_ref):
    ang = inv_ref[...] * pos_ref[0]
    cos_t = jnp.cos(ang)
    sin_t = jnp.sin(ang)
    cct = jnp.concatenate([cos_t, cos_t], axis=0)
    sst = jnp.concatenate([-sin_t, sin_t], axis=0)
    cct_ref[0] = cct
    sst_ref[0] = sst
    cc_ref[0] = jnp.concatenate([cct, cct], axis=0).T
    ss_ref[0] = jnp.concatenate([sst, sst], axis=0).T


def _rope_tables(positions):
    b, s = positions.shape
    half = QK_ROPE // 2
    inv = 1.0 / (ROPE_THETA ** (jnp.arange(0, QK_ROPE, 2, dtype=F32) / QK_ROPE))
    pos = positions.astype(F32)[:, None, :]
    ts = ROPE_TILE
    tab = pl.BlockSpec((1, ts, LANES), lambda i, j: (i, j, 0))
    tab_t = pl.BlockSpec((1, QK_ROPE, ts), lambda i, j: (i, 0, j))
    return pl.pallas_call(
        _rope_table_kernel,
        grid=(b, s // ts),
        in_specs=[pl.BlockSpec((1, 1, ts), lambda i, j: (i, 0, j)),
                  pl.BlockSpec((half, 1), lambda i, j: (0, 0))],
        out_specs=[tab, tab, tab_t, tab_t],
        out_shape=[jax.ShapeDtypeStruct((b, s, LANES), F32)] * 2
                  + [jax.ShapeDtypeStruct((b, QK_ROPE, s), F32)] * 2,
        compiler_params=_params(("parallel", "parallel")),
        name="rope_tables",
    )(pos, inv[:, None])


def _mods_kernel(c_ref, w_ref, b_ref, o_ref):
    c = c_ref[...]
    sc = (c * jax.nn.sigmoid(c)).astype(BF16)
    o_ref[0] = _dot(sc, w_ref[0].astype(BF16)) + b_ref[0]


def _mods(c, mod_w, mod_b):
    depth, d, n = mod_w.shape
    b = c.shape[0]
    nt = MODS_N_TILE
    return pl.pallas_call(
        _mods_kernel,
        grid=(depth, n // nt),
        in_specs=[pl.BlockSpec((b, d), lambda l, j: (0, 0)),
                  pl.BlockSpec((1, d, nt), lambda l, j: (l, 0, j)),
                  pl.BlockSpec((1, 1, nt), lambda l, j: (l, 0, j))],
        out_specs=pl.BlockSpec((1, b, nt), lambda l, j: (l, 0, j)),
        out_shape=jax.ShapeDtypeStruct((depth, b, n), F32),
        compiler_params=_params(("parallel", "parallel")),
        name="adaln_mods",
    )(c, mod_w, mod_b.reshape(depth, 1, n))


def _shift_rows(a, k, prev_rows):
    rolled = pltpu.roll(a, k, axis=0)
    prev = pltpu.roll(prev_rows, k, axis=0)
    row = lax.broadcasted_iota(jnp.int32, prev_rows.shape, 0)
    head = jnp.where(row < k, prev, rolled[0:SUBLANES])
    return jnp.concatenate([head, rolled[SUBLANES:]], axis=0)


def _pooled_groups(h, halo, first_pos):
    rows, d = h.shape
    group = d // len(POOL_WINDOWS)
    t = first_pos + lax.broadcasted_iota(jnp.int32, (rows, 1), 0)
    pooled = []
    for gi, w in enumerate(POOL_WINDOWS):
        lo = gi * group
        hg = h[:, lo:lo + group]
        win = jnp.concatenate([halo[:, lo:lo + group], hg], axis=0)
        k = 1
        while k < w:
            win = win + pltpu.roll(win, k, axis=0)
            k *= 2
        cnt = jnp.minimum(t + 1, w).astype(F32)
        pooled.append((win[POOL_HALO:] / cnt - hg).astype(BF16))
    return pooled


def _layer_kernel(*refs, mixer, with_final):
    it = iter(refs)
    x_ref = next(it)
    mod_ref = next(it)
    if mixer == "attn":
        o_in_ref = next(it)
        wo_ref = next(it)
    else:
        g1_ref = next(it)
        pw_ref = next(it)
        pb_ref = next(it)
        ps_ref = next(it)
    g2_ref = next(it)
    wup_ref = next(it)
    cw_ref = next(it)
    cb_ref = next(it)
    wd_ref = next(it)
    if with_final:
        fg_ref = next(it)
    out_ref = next(it)
    carry_ref = next(it)
    gate_ref = next(it)
    if mixer == "pool":
        halo_ref = next(it)

    s = pl.program_id(1)
    ts = x_ref.shape[1]
    sub = ts // LAYER_SUBTILES
    f = wd_ref.shape[0]
    fc = FF_CHUNK
    n_chunks = f // fc

    @pl.when(s == 0)
    def _():
        carry_ref[...] = jnp.zeros(carry_ref.shape, F32)
        if mixer == "pool":
            halo_ref[...] = jnp.zeros(halo_ref.shape, F32)

    mod = mod_ref[0]

    halo = halo_ref[...] if mixer == "pool" else None

    def mixer_vector_part(t, anchor=None):
        nonlocal halo
        if mixer == "attn":
            return None
        x = x_ref[0, t * sub:(t + 1) * sub]
        if anchor is not None:
            x = x + anchor
        h1 = _rmsnorm(x, g1_ref[...]) * (1.0 + mod[1:2]) + mod[0:1]
        pooled = _pooled_groups(h1, halo, s * ts + t * sub)
        halo = h1[sub - POOL_HALO:sub]
        return pooled

    def mixer_matmul_part(t, pooled):
        rows = slice(t * sub, (t + 1) * sub)
        if mixer == "attn":
            o = jnp.concatenate([o_in_ref[0, hd, rows, :] for hd in range(N_HEADS)], axis=-1)
            y = _dot(o, wo_ref[...])
        else:
            y = jnp.concatenate([_dot(p, pw_ref[gi]) for gi, p in enumerate(pooled)], axis=-1)
            y = (y + pb_ref[...]) * ps_ref[...]
        x1 = x_ref[0, rows] + mod[2:3] * y
        out_ref[0, rows] = x1
        return (_rmsnorm(x1, g2_ref[...]) * (1.0 + mod[4:5]) + mod[3:4]).astype(BF16)

    def zero_after(value):
        bits = lax.bitcast_convert_type(value[0:1, 0:LANES], jnp.uint32)
        bits = lax.shift_right_logical(lax.shift_right_logical(bits, jnp.uint32(16)), jnp.uint32(16))
        zeros = lax.bitcast_convert_type(bits, F32)
        return jnp.concatenate([zeros] * (x_ref.shape[2] // LANES), axis=1)

    sqrt_half = math.sqrt(0.5)
    tails = [carry_ref[:, j * fc:(j + 1) * fc] for j in range(n_chunks)]
    h_next = mixer_matmul_part(0, mixer_vector_part(0))
    for t in range(LAYER_SUBTILES):
        rows = slice(t * sub, (t + 1) * sub)
        h = h_next
        up = lambda j: (_dot(h, wup_ref[:, j * fc:(j + 1) * fc]),
                        _dot(h, wup_ref[:, f + j * fc:f + (j + 1) * fc]))
        nxt = up(0)
        if t + 1 < LAYER_SUBTILES:
            pooled_next = mixer_vector_part(t + 1, anchor=zero_after(nxt[0]))
        for j in range(n_chunks):
            cols = slice(j * fc, (j + 1) * fc)
            a, v = nxt
            if j + 1 < n_chunks:
                nxt = up(j + 1)
            a1 = _shift_rows(a, 1, tails[j])
            a2 = _shift_rows(a, 2, tails[j])
            conv = (a2 * cw_ref[0:1, cols] + a1 * cw_ref[1:2, cols] + a * cw_ref[2:3, cols]
                    + cb_ref[:, cols])
            gate = 0.5 * conv * (1.0 + lax.erf(conv * sqrt_half))
            gate_ref[:, cols] = (gate * v).astype(BF16)
            tails[j] = a[sub - SUBLANES:sub]
        if t + 1 < LAYER_SUBTILES:
            h_next = mixer_matmul_part(t + 1, pooled_next)
        y = out_ref[0, rows] + mod[5:6] * _dot(gate_ref[...], wd_ref[...])
        if with_final:
            y = _rmsnorm(y, fg_ref[...])
        out_ref[0, rows] = y
    for j in range(n_chunks):
        carry_ref[:, j * fc:(j + 1) * fc] = tails[j]
    if mixer == "pool":
        halo_ref[...] = halo


def _layer_tail(x, mods_l, g2, wup_all, cw, cb, wd_all, layer, attn=None, pool=None, final_g=None):
    b, s, d = x.shape
    ts = LAYER_SUBTILES * SEQ_TILE
    f = wd_all.shape[1]
    assert f % FF_CHUNK == 0 and s % ts == 0 and (attn is None) != (pool is None)
    layer_block = lambda i, j: (layer, 0, 0)
    const2 = lambda i, j: (0, 0)
    vec = pl.BlockSpec((1, d), const2)
    tile = pl.BlockSpec((1, ts, d), lambda i, j: (i, j, 0))
    single = dict(pipeline_mode=pl.Buffered(1))
    in_specs = [tile, pl.BlockSpec((1, N_MOD, d), lambda i, j: (i, 0, 0))]
    args = [x, mods_l]
    scratch = [pltpu.VMEM((SUBLANES, f), F32), pltpu.VMEM((SEQ_TILE, f), BF16)]
    if attn is not None:
        o, wo = attn
        in_specs += [pl.BlockSpec((1, o.shape[1], ts, o.shape[3]), lambda i, j: (i, 0, j, 0)),
                     pl.BlockSpec(wo.shape, const2, **single)]
        args += [o, wo]
    else:
        g1, pw, pb, ps = pool
        in_specs += [vec, pl.BlockSpec(pw.shape, lambda i, j: (0, 0, 0)), vec, vec]
        args += [g1, pw, pb, ps]
        scratch.append(pltpu.VMEM((POOL_HALO, d), F32))
    in_specs += [vec,
                 pl.BlockSpec((None,) + wup_all.shape[1:], layer_block, **single),
                 pl.BlockSpec(cw.shape, const2),
                 pl.BlockSpec(cb.shape, const2),
                 pl.BlockSpec((None,) + wd_all.shape[1:], layer_block, **single)]
    args += [g2, wup_all, cw, cb, wd_all]
    if final_g is not None:
        in_specs.append(vec)
        args.append(final_g)
    kern = functools.partial(_layer_kernel, mixer="attn" if attn is not None else "pool",
                             with_final=final_g is not None)
    return pl.pallas_call(
        kern,
        grid=(b, s // ts),
        in_specs=in_specs,
        out_specs=tile,
        out_shape=jax.ShapeDtypeStruct(x.shape, F32),
        scratch_shapes=scratch,
        compiler_params=_params(("arbitrary", "arbitrary")),
        name="layer_tail_" + ("attn" if attn is not None else "pool"),
    )(*args)


def _kv_kernel(x_ref, g_ref, wdkv_ref, cg_ref, wuk_ref, wuvt_ref, cc_ref, ss_ref, k_ref, vt_ref):
    ts = x_ref.shape[1]
    kv_rank = cg_ref.shape[1]
    xn = _rmsnorm(x_ref[0], g_ref[...]).astype(BF16)
    kv = _dot(xn, wdkv_ref[...])
    c_kv = _rmsnorm(kv[:, 0:kv_rank], cg_ref[...]).astype(BF16)
    kr = kv[:, kv_rank:kv_rank + LANES]
    k_rope = kr * cc_ref[0] + pltpu.roll(kr, QK_ROPE // 2, axis=1) * ss_ref[0]
    k_rope = k_rope[:, 0:QK_ROPE].astype(BF16)
    row = lax.broadcasted_iota(jnp.int32, (V_EXT - V_HEAD, ts), 0)
    ones_row = jnp.where(row == 0, 1.0, 0.0).astype(BF16)
    k_nope = _dot(c_kv, wuk_ref[...])
    vt = lax.dot_general(wuvt_ref[...], c_kv, (((1,), (1,)), ((), ())), preferred_element_type=F32)
    for hd in range(N_HEADS):
        k_ref[0, hd, :, 0:QK_NOPE] = k_nope[:, hd * QK_NOPE:(hd + 1) * QK_NOPE].astype(BF16)
        k_ref[0, hd, :, QK_NOPE:QK_HEAD] = k_rope
        k_ref[0, hd, :, QK_HEAD:QK_PAD] = jnp.zeros((ts, QK_PAD - QK_HEAD), BF16)
        vt_ref[0, hd, 0:V_HEAD, :] = vt[hd * V_HEAD:(hd + 1) * V_HEAD, :].astype(BF16)
        vt_ref[0, hd, V_HEAD:V_EXT, :] = ones_row


def _shared_kv(x, g, wdkv, cg, wuk, wuvt, cc, ss):
    b, s, d = x.shape
    ts = PROJ_TILE
    const2 = lambda i, j: (0, 0)
    full = lambda a: pl.BlockSpec(a.shape, const2)
    tab = pl.BlockSpec((1, ts, LANES), lambda i, j: (i, j, 0))
    return pl.pallas_call(
        _kv_kernel,
        grid=(b, s // ts),
        in_specs=[pl.BlockSpec((1, ts, d), lambda i, j: (i, j, 0)),
                  full(g), full(wdkv), full(cg), full(wuk), full(wuvt), tab, tab],
        out_specs=[pl.BlockSpec((1, N_HEADS, ts, QK_PAD), lambda i, j: (i, 0, j, 0)),
                   pl.BlockSpec((1, N_HEADS, V_EXT, ts), lambda i, j: (i, 0, 0, j))],
        out_shape=[jax.ShapeDtypeStruct((b, N_HEADS, s, QK_PAD), BF16),
                   jax.ShapeDtypeStruct((b, N_HEADS, V_EXT, s), BF16)],
        compiler_params=_params(("parallel", "parallel")),
        name="shared_kv",
    )(x, g, wdkv, cg, wuk, wuvt, cc, ss)


def _q_kernel(x_ref, mod_ref, g_ref, wdq_ref, qg_ref, wuqt_ref, cct_ref, sst_ref, qt_ref):
    mod = mod_ref[0]
    h = (_rmsnorm(x_ref[0], g_ref[...]) * (1.0 + mod[1:2]) + mod[0:1]).astype(BF16)
    c_q = _rmsnorm(_dot(h, wdq_ref[...]), qg_ref[...]).astype(BF16)
    scale = QK_HEAD ** -0.5 * math.log2(math.e)
    n_nope = N_HEADS * QK_NOPE
    n_rope = N_HEADS * QK_ROPE
    qt = lax.dot_general(wuqt_ref[...], c_q, (((1,), (1,)), ((), ())), preferred_element_type=F32)
    cct = jnp.concatenate([cct_ref[0]] * N_HEADS, axis=0)
    sst = jnp.concatenate([sst_ref[0]] * N_HEADS, axis=0)
    q_nope = qt[0:n_nope] * scale
    qr = qt[n_nope:n_nope + n_rope]
    half = QK_ROPE // 2
    qr_swapped = jnp.concatenate(
        [qr[hd * QK_ROPE + off:hd * QK_ROPE + off + half] for hd in range(N_HEADS) for off in (half, 0)],
        axis=0)
    q_rope = (qr * cct + qr_swapped * sst) * scale
    for hd in range(N_HEADS):
        qt_ref[0, hd, 0:QK_NOPE, :] = q_nope[hd * QK_NOPE:(hd + 1) * QK_NOPE].astype(BF16)
        qt_ref[0, hd, QK_NOPE:QK_HEAD, :] = q_rope[hd * QK_ROPE:(hd + 1) * QK_ROPE].astype(BF16)
        qt_ref[0, hd, QK_HEAD:QK_PAD, :] = jnp.zeros((QK_PAD - QK_HEAD, qt.shape[1]), BF16)


def _q_proj(x, mods_l, g, wdq, qg, wuqt, cct, sst):
    b, s, d = x.shape
    ts = PROJ_TILE
    const2 = lambda i, j: (0, 0)
    full = lambda a: pl.BlockSpec(a.shape, const2)
    tab_t = pl.BlockSpec((1, QK_ROPE, ts), lambda i, j: (i, 0, j))
    return pl.pallas_call(
        _q_kernel,
        grid=(b, s // ts),
        in_specs=[pl.BlockSpec((1, ts, d), lambda i, j: (i, j, 0)),
                  pl.BlockSpec((1, N_MOD, d), lambda i, j: (i, 0, 0)),
                  full(g), full(wdq), full(qg), full(wuqt), tab_t, tab_t],
        out_specs=pl.BlockSpec((1, N_HEADS, QK_PAD, ts), lambda i, j: (i, 0, 0, j)),
        out_shape=jax.ShapeDtypeStruct((b, N_HEADS, QK_PAD, s), BF16),
        compiler_params=_params(("parallel", "parallel")),
        name="q_proj",
    )(x, mods_l, g, wdq, qg, wuqt, cct, sst)


def _attn_kernel(qt_ref, k_ref, vt_ref, o_ref, acc_ref, m_ref, s0_ref, s1_ref, s2_ref,
                 p0_ref, p1_ref):
    s_len = qt_ref.shape[3]
    tk, qb = s0_ref.shape
    s_refs = (s0_ref, s1_ref, s2_ref)
    p_refs = (p0_ref, p1_ref)

    units = []
    for d in range(s_len // tk):
        c0 = d * tk
        while c0 < s_len:
            c1 = min((c0 // qb + 1) * qb, s_len)
            units.append((d, c0, c1))
            c0 = c1

    def scores(u):
        d, c0, c1 = units[u]
        st = _dot(k_ref[0, 0, d * tk:(d + 1) * tk, :], qt_ref[0, 0, :, c0:c1])
        w_diag = min(c1, (d + 1) * tk) - c0
        if w_diag > 0:
            key = lax.broadcasted_iota(jnp.int32, (tk, w_diag), 0) + d * tk
            qry = lax.broadcasted_iota(jnp.int32, (tk, w_diag), 1) + c0
            tri = jnp.where(key <= qry, st[:, :w_diag], jnp.finfo(F32).min)
            st = tri if w_diag == c1 - c0 else jnp.concatenate([tri, st[:, w_diag:]], axis=1)
        s_refs[u % 3][:, :c1 - c0] = st
        return jnp.max(st, axis=0, keepdims=True)

    def softmax_step(u, st_max):
        d, c0, c1 = units[u]
        m_prev = m_ref[:, c0:c1]
        m_new = jnp.maximum(m_prev, st_max)
        alpha = jnp.exp2(m_prev - m_new)
        p = jnp.exp2(s_refs[u % 3][:, :c1 - c0] - m_new)
        p_refs[u % 2][:, :c1 - c0] = p.astype(BF16)
        m_ref[:, c0:c1] = m_new
        return alpha

    def accumulate(u, alpha):
        d, c0, c1 = units[u]
        pv = _dot(vt_ref[0, 0, :, d * tk:(d + 1) * tk], p_refs[u % 2][:, :c1 - c0])
        if d == 0:
            acc_ref[:, c0:c1] = pv
        else:
            acc_ref[:, c0:c1] = alpha * acc_ref[:, c0:c1] + pv

    m_ref[...] = jnp.full(m_ref.shape, -jnp.inf, F32)
    ahead = ATTN_SCORES_AHEAD
    maxes = {u: scores(u) for u in range(ahead)}
    alpha_prev = None
    for u in range(len(units)):
        if u + ahead < len(units):
            maxes[u + ahead] = scores(u + ahead)
        alpha = softmax_step(u, maxes.pop(u))
        if u:
            accumulate(u - 1, alpha_prev)
        alpha_prev = alpha
    accumulate(len(units) - 1, alpha_prev)
    o_ref[0, 0] = (acc_ref[0:V_HEAD, :] / acc_ref[V_HEAD:V_HEAD + 1, :]).T.astype(o_ref.dtype)


def _attention(qt, k, vt):
    b, nh, dk, s = qt.shape
    dv = vt.shape[2]
    assert dv == V_EXT
    tk = ATTN_KEY_CHUNK
    qb = ATTN_QUERY_BLOCK
    assert s % qb == 0 and s % tk == 0
    return pl.pallas_call(
        _attn_kernel,
        grid=(b, nh),
        in_specs=[pl.BlockSpec((1, 1, dk, s), lambda i, h: (i, h, 0, 0)),
                  pl.BlockSpec((1, 1, s, dk), lambda i, h: (i, h, 0, 0)),
                  pl.BlockSpec((1, 1, dv, s), lambda i, h: (i, h, 0, 0))],
        out_specs=pl.BlockSpec((1, 1, s, V_HEAD), lambda i, h: (i, h, 0, 0)),
        out_shape=jax.ShapeDtypeStruct((b, nh, s, V_HEAD), BF16),
        scratch_shapes=[pltpu.VMEM((dv, s), F32), pltpu.VMEM((1, s), F32),
                        *[pltpu.VMEM((tk, qb), F32)] * 3, *[pltpu.VMEM((tk, qb), BF16)] * 2],
        compiler_params=_params(("parallel", "parallel")),
        name="causal_attention",
    )(qt, k, vt)


def _prep_wuq(w_uq):
    r = w_uq.shape[0]
    w = w_uq.reshape(r, N_HEADS, QK_HEAD)
    nope = w[:, :, :QK_NOPE].reshape(r, N_HEADS * QK_NOPE)
    rope = w[:, :, QK_NOPE:].reshape(r, N_HEADS * QK_ROPE)
    return jnp.concatenate([nope, rope], axis=-1).T.astype(BF16)


def kernel(x, c, positions, mod_w, mod_b, norm1_g, norm2_g, pool_w, pool_b, pool_scale, kv_in_g,
           w_dkv, ckv_norm_g, w_uk, w_uv, w_dq, q_norm_g, w_uq, w_o, w_up, conv_w, conv_b, w_down,
           final_g):
    b, s, d = x.shape
    depth = mod_w.shape[0]
    n_pool = pool_w.shape[0]
    kv_rank = ckv_norm_g.shape[0]

    cc, ss, cct, sst = _rope_tables(positions)
    mods = _mods(c, mod_w, mod_b).reshape(depth, b, N_MOD, d)
    row = lambda a: a.reshape(1, -1)

    w_up_b = w_up.astype(BF16)
    w_down_b = w_down.astype(BF16)
    k = vt = None
    for l in range(depth):
        ffn_w = (w_up_b, conv_w[l], row(conv_b[l]), w_down_b, l)
        final = row(final_g) if l == depth - 1 else None
        if l < n_pool:
            pool = (row(norm1_g[l]), pool_w[l].astype(BF16), row(pool_b[l]), row(pool_scale[l]))
            x = _layer_tail(x, mods[l], row(norm2_g[l]), *ffn_w, pool=pool, final_g=final)
        else:
            j = l - n_pool
            q = _q_proj(x, mods[l], row(norm1_g[l]), w_dq[j].astype(BF16), row(q_norm_g[j]),
                        _prep_wuq(w_uq[j]), cct, sst)
            o = _attention(q, k, vt)
            x = _layer_tail(x, mods[l], row(norm2_g[l]), *ffn_w, attn=(o, w_o[j].astype(BF16)),
                            final_g=final)
        if l == n_pool - 1:
            wdkv = jnp.concatenate([w_dkv, w_dkv[:, kv_rank:]], axis=1).astype(BF16)
            k, vt = _shared_kv(x, row(kv_in_g), wdkv, row(ckv_norm_g), w_uk.astype(BF16),
                               w_uv.T.astype(BF16), cc, ss)
    return x
```

```python
import functools
import math

import jax
import jax.numpy as jnp
from jax import lax
from jax.experimental import pallas as pl
from jax.experimental.pallas import tpu as pltpu

F32 = jnp.float32
BF16 = jnp.bfloat16

POOL_WINDOWS = (2, 4, 8, 16)
N_HEADS = 8
QK_NOPE = 128
QK_ROPE = 64
V_HEAD = 128
QK_HEAD = QK_NOPE + QK_ROPE
QK_PAD = 256
V_EXT = V_HEAD + 16
ROPE_THETA = 10000.0
CONV_WIDTH = 3
EPS = 1e-6
N_MOD = 6

LANES = 128
SUBLANES = 8
VMEM_LIMIT_BYTES = 56 * 1024 * 1024

SEQ_TILE = 256
LAYER_SUBTILES = 4
PROJ_TILE = 1024
ROPE_TILE = 2048
FF_CHUNK = 256
ATTN_KEY_CHUNK = 256
ATTN_QUERY_BLOCK = 512
ATTN_SCORES_AHEAD = 1
POOL_HALO = 16
MODS_N_TILE = 1536


def _params(sem):
    return pltpu.CompilerParams(dimension_semantics=sem, vmem_limit_bytes=VMEM_LIMIT_BYTES)


def _rmsnorm(x, g):
    ms = jnp.mean(x * x, axis=-1, keepdims=True)
    return x * lax.rsqrt(ms + EPS) * g


def _dot(a, b):
    return jnp.dot(a, b, preferred_element_type=F32)


def _rope_table_kernel(pos_ref, inv_ref, cc_ref, ss_ref, cct_ref, sst_ref):
    ang = inv_ref[...] * pos_ref[0]
    cos_t = jnp.cos(ang)
    sin_t = jnp.sin(ang)
    cct = jnp.concatenate([cos_t, cos_t], axis=0)
    sst = jnp.concatenate([-sin_t, sin_t], axis=0)
    cct_ref[0] = cct
    sst_ref[0] = sst
    cc_ref[0] = jnp.concatenate([cct, cct], axis=0).T
    ss_ref[0] = jnp.concatenate([sst, sst], axis=0).T


def _rope_tables(positions):
    b, s = positions.shape
    half = QK_ROPE // 2
    inv = 1.0 / (ROPE_THETA ** (jnp.arange(0, QK_ROPE, 2, dtype=F32) / QK_ROPE))
    pos = positions.astype(F32)[:, None, :]
    ts = ROPE_TILE
    tab = pl.BlockSpec((1, ts, LANES), lambda i, j: (i, j, 0))
    tab_t = pl.BlockSpec((1, QK_ROPE, ts), lambda i, j: (i, 0, j))
    return pl.pallas_call(
        _rope_table_kernel,
        grid=(b, s // ts),
        in_specs=[pl.BlockSpec((1, 1, ts), lambda i, j: (i, 0, j)),
                  pl.BlockSpec((half, 1), lambda i, j: (0, 0))],
        out_specs=[tab, tab, tab_t, tab_t],
        out_shape=[jax.ShapeDtypeStruct((b, s, LANES), F32)] * 2
                  + [jax.ShapeDtypeStruct((b, QK_ROPE, s), F32)] * 2,
        compiler_params=_params(("parallel", "parallel")),
        name="rope_tables",
    )(pos, inv[:, None])


def _mods_kernel(c_ref, w_ref, b_ref, o_ref):
    c = c_ref[...]
    sc = (c * jax.nn.sigmoid(c)).astype(BF16)
    o_ref[0] = _dot(sc, w_ref[0].astype(BF16)) + b_ref[0]


def _mods(c, mod_w, mod_b):
    depth, d, n = mod_w.shape
    b = c.shape[0]
    nt = MODS_N_TILE
    return pl.pallas_call(
        _mods_kernel,
        grid=(depth, n // nt),
        in_specs=[pl.BlockSpec((b, d), lambda l, j: (0, 0)),
                  pl.BlockSpec((1, d, nt), lambda l, j: (l, 0, j)),
                  pl.BlockSpec((1, 1, nt), lambda l, j: (l, 0, j))],
        out_specs=pl.BlockSpec((1, b, nt), lambda l, j: (l, 0, j)),
        out_shape=jax.ShapeDtypeStruct((depth, b, n), F32),
        compiler_params=_params(("parallel", "parallel")),
        name="adaln_mods",
    )(c, mod_w, mod_b.reshape(depth, 1, n))


def _shift_rows(a, k, prev_rows):
    rolled = pltpu.roll(a, k, axis=0)
    prev = pltpu.roll(prev_rows, k, axis=0)
    row = lax.broadcasted_iota(jnp.int32, prev_rows.shape, 0)
    head = jnp.where(row < k, prev, rolled[0:SUBLANES])
    return jnp.concatenate([head, rolled[SUBLANES:]], axis=0)


def _pooled_groups(h, halo, first_pos):
    rows, d = h.shape
    group = d // len(POOL_WINDOWS)
    t = first_pos + lax.broadcasted_iota(jnp.int32, (rows, 1), 0)
    pooled = []
    for gi, w in enumerate(POOL_WINDOWS):
        lo = gi * group
        hg = h[:, lo:lo + group]
        win = jnp.concatenate([halo[:, lo:lo + group], hg], axis=0)
        k = 1
        while k < w:
            win = win + pltpu.roll(win, k, axis=0)
            k *= 2
        cnt = jnp.minimum(t + 1, w).astype(F32)
        pooled.append((win[POOL_HALO:] / cnt - hg).astype(BF16))
    return pooled


def _layer_kernel(*refs, mixer, with_final):
    it = iter(refs)
    x_ref = next(it)
    mod_ref = next(it)
    if mixer == "attn":
        o_in_ref = next(it)
        wo_ref = next(it)
    else:
        g1_ref = next(it)
        pw_ref = next(it)
        pb_ref = next(it)
        ps_ref = next(it)
    g2_ref = next(it)
    wup_ref = next(it)
    cw_ref = next(it)
    cb_ref = next(it)
    wd_ref = next(it)
    if with_final:
        fg_ref = next(it)
    out_ref = next(it)
    carry_ref = next(it)
    gate_ref = next(it)
    if mixer == "pool":
        halo_ref = next(it)

    s = pl.program_id(1)
    ts = x_ref.shape[1]
    sub = ts // LAYER_SUBTILES
    f = wd_ref.shape[0]
    fc = FF_CHUNK
    n_chunks = f // fc

    @pl.when(s == 0)
    def _():
        carry_ref[...] = jnp.zeros(carry_ref.shape, F32)
        if mixer == "pool":
            halo_ref[...] = jnp.zeros(halo_ref.shape, F32)

    mod = mod_ref[0]

    halo = halo_ref[...] if mixer == "pool" else None

    def mixer_vector_part(t, anchor=None):
        nonlocal halo
        if mixer == "attn":
            return None
        x = x_ref[0, t * sub:(t + 1) * sub]
        if anchor is not None:
            x = x + anchor
        h1 = _rmsnorm(x, g1_ref[...]) * (1.0 + mod[1:2]) + mod[0:1]
        pooled = _pooled_groups(h1, halo, s * ts + t * sub)
        halo = h1[sub - POOL_HALO:sub]
        return pooled

    def mixer_matmul_part(t, pooled):
        rows = slice(t * sub, (t + 1) * sub)
        if mixer == "attn":
            o = jnp.concatenate([o_in_ref[0, hd, rows, :] for hd in range(N_HEADS)], axis=-1)
            y = _dot(o, wo_ref[...])
        else:
            y = jnp.concatenate([_dot(p, pw_ref[gi]) for gi, p in enumerate(pooled)], axis=-1)
            y = (y + pb_ref[...]) * ps_ref[...]
        x1 = x_ref[0, rows] + mod[2:3] * y
        out_ref[0, rows] = x1
        return (_rmsnorm(x1, g2_ref[...]) * (1.0 + mod[4:5]) + mod[3:4]).astype(BF16)

    def zero_after(value):
        bits = lax.bitcast_convert_type(value[0:1, 0:LANES], jnp.uint32)
        bits = lax.shift_right_logical(lax.shift_right_logical(bits, jnp.uint32(16)), jnp.uint32(16))
        zeros = lax.bitcast_convert_type(bits, F32)
        return jnp.concatenate([zeros] * (x_ref.shape[2] // LANES), axis=1)

    sqrt_half = math.sqrt(0.5)
    tails = [carry_ref[:, j * fc:(j + 1) * fc] for j in range(n_chunks)]
    h_next = mixer_matmul_part(0, mixer_vector_part(0))
    for t in range(LAYER_SUBTILES):
        rows = slice(t * sub, (t + 1) * sub)
        h = h_next
        up = lambda j: (_dot(h, wup_ref[:, j * fc:(j + 1) * fc]),
                        _dot(h, wup_ref[:, f + j * fc:f + (j + 1) * fc]))
        nxt = up(0)
        if t + 1 < LAYER_SUBTILES:
            pooled_next = mixer_vector_part(t + 1, anchor=zero_after(nxt[0]))
        for j in range(n_chunks):
            cols = slice(j * fc, (j + 1) * fc)
            a, v = nxt
            if j + 1 < n_chunks:
                nxt = up(j + 1)
            a1 = _shift_rows(a, 1, tails[j])
            a2 = _shift_rows(a, 2, tails[j])
            conv = (a2 * cw_ref[0:1, cols] + a1 * cw_ref[1:2, cols] + a * cw_ref[2:3, cols]
                    + cb_ref[:, cols])
            gate = 0.5 * conv * (1.0 + lax.erf(conv * sqrt_half))
            gate_ref[:, cols] = (gate * v).astype(BF16)
            tails[j] = a[sub - SUBLANES:sub]
        if t + 1 < LAYER_SUBTILES:
            h_next = mixer_matmul_part(t + 1, pooled_next)
        y = out_ref[0, rows] + mod[5:6] * _dot(gate_ref[...], wd_ref[...])
        if with_final:
            y = _rmsnorm(y, fg_ref[...])
        out_ref[0, rows] = y
    for j in range(n_chunks):
        carry_ref[:, j * fc:(j + 1) * fc] = tails[j]
    if mixer == "pool":
        halo_ref[...] = halo


def _layer_tail(x, mods_l, g2, wup_all, cw, cb, wd_all, layer, attn=None, pool=None, final_g=None):
    b, s, d = x.shape
    ts = LAYER_SUBTILES * SEQ_TILE
    f = wd_all.shape[1]
    assert f % FF_CHUNK == 0 and s % ts == 0 and (attn is None) != (pool is None)
    layer_block = lambda i, j: (layer, 0, 0)
    const2 = lambda i, j: (0, 0)
    vec = pl.BlockSpec((1, d), const2)
    tile = pl.BlockSpec((1, ts, d), lambda i, j: (i, j, 0))
    single = dict(pipeline_mode=pl.Buffered(1))
    in_specs = [tile, pl.BlockSpec((1, N_MOD, d), lambda i, j: (i, 0, 0))]
    args = [x, mods_l]
    scratch = [pltpu.VMEM((SUBLANES, f), F32), pltpu.VMEM((SEQ_TILE, f), BF16)]
    if attn is not None:
        o, wo = attn
        in_specs += [pl.BlockSpec((1, o.shape[1], ts, o.shape[3]), lambda i, j: (i, 0, j, 0)),
                     pl.BlockSpec(wo.shape, const2, **single)]
        args += [o, wo]
    else:
        g1, pw, pb, ps = pool
        in_specs += [vec, pl.BlockSpec(pw.shape, lambda i, j: (0, 0, 0)), vec, vec]
        args += [g1, pw, pb, ps]
        scratch.append(pltpu.VMEM((POOL_HALO, d), F32))
    in_specs += [vec,
                 pl.BlockSpec((None,) + wup_all.shape[1:], layer_block, **single),
                 pl.BlockSpec(cw.shape, const2),
                 pl.BlockSpec(cb.shape, const2),
                 pl.BlockSpec((None,) + wd_all.shape[1:], layer_block, **single)]
    args += [g2, wup_all, cw, cb, wd_all]
    if final_g is not None:
        in_specs.append(vec)
        args.append(final_g)
    kern = functools.partial(_layer_kernel, mixer="attn" if attn is not None else "pool",
                             with_final=final_g is not None)
    return pl.pallas_call(
        kern,
        grid=(b, s // ts),
        in_specs=in_specs,
        out_specs=tile,
        out_shape=jax.ShapeDtypeStruct(x.shape, F32),
        scratch_shapes=scratch,
        compiler_params=_params(("arbitrary", "arbitrary")),
        name="layer_tail_" + ("attn" if attn is not None else "pool"),
    )(*args)


def _kv_kernel(x_ref, g_ref, wdkv_ref, cg_ref, wuk_ref, wuvt_ref, cc_ref, ss_ref, k_ref, vt_ref):
    ts = x_ref.shape[1]
    kv_rank = cg_ref.shape[1]
    xn = _rmsnorm(x_ref[0], g_ref[...]).astype(BF16)
    kv = _dot(xn, wdkv_ref[...])
    c_kv = _rmsnorm(kv[:, 0:kv_rank], cg_ref[...]).astype(BF16)
    kr = kv[:, kv_rank:kv_rank + LANES]
    k_rope = kr * cc_ref[0] + pltpu.roll(kr, QK_ROPE // 2, axis=1) * ss_ref[0]
    k_rope = k_rope[:, 0:QK_ROPE].astype(BF16)
    row = lax.broadcasted_iota(jnp.int32, (V_EXT - V_HEAD, ts), 0)
    ones_row = jnp.where(row == 0, 1.0, 0.0).astype(BF16)
    k_nope = _dot(c_kv, wuk_ref[...])
    vt = lax.dot_general(wuvt_ref[...], c_kv, (((1,), (1,)), ((), ())), preferred_element_type=F32)
    for hd in range(N_HEADS):
        k_ref[0, hd, :, 0:QK_NOPE] = k_nope[:, hd * QK_NOPE:(hd + 1) * QK_NOPE].astype(BF16)
        k_ref[0, hd, :, QK_NOPE:QK_HEAD] = k_rope
        k_ref[0, hd, :, QK_HEAD:QK_PAD] = jnp.zeros((ts, QK_PAD - QK_HEAD), BF16)
        vt_ref[0, hd, 0:V_HEAD, :] = vt[hd * V_HEAD:(hd + 1) * V_HEAD, :].astype(BF16)
        vt_ref[0, hd, V_HEAD:V_EXT, :] = ones_row


def _shared_kv(x, g, wdkv, cg, wuk, wuvt, cc, ss):
    b, s, d = x.shape
    ts = PROJ_TILE
    const2 = lambda i, j: (0, 0)
    full = lambda a: pl.BlockSpec(a.shape, const2)
    tab = pl.BlockSpec((1, ts, LANES), lambda i, j: (i, j, 0))
    return pl.pallas_call(
        _kv_kernel,
        grid=(b, s // ts),
        in_specs=[pl.BlockSpec((1, ts, d), lambda i, j: (i, j, 0)),
                  full(g), full(wdkv), full(cg), full(wuk), full(wuvt), tab, tab],
        out_specs=[pl.BlockSpec((1, N_HEADS, ts, QK_PAD), lambda i, j: (i, 0, j, 0)),
                   pl.BlockSpec((1, N_HEADS, V_EXT, ts), lambda i, j: (i, 0, 0, j))],
        out_shape=[jax.ShapeDtypeStruct((b, N_HEADS, s, QK_PAD), BF16),
                   jax.ShapeDtypeStruct((b, N_HEADS, V_EXT, s), BF16)],
        compiler_params=_params(("parallel", "parallel")),
        name="shared_kv",
    )(x, g, wdkv, cg, wuk, wuvt, cc, ss)


def _q_kernel(x_ref, mod_ref, g_ref, wdq_ref, qg_ref, wuqt_ref, cct_ref, sst_ref, qt_ref):
    mod = mod_ref[0]
    h = (_rmsnorm(x_ref[0], g_ref[...]) * (1.0 + mod[1:2]) + mod[0:1]).astype(BF16)
    c_q = _rmsnorm(_dot(h, wdq_ref[...]), qg_ref[...]).astype(BF16)
    scale = QK_HEAD ** -0.5 * math.log2(math.e)
    n_nope = N_HEADS * QK_NOPE
    n_rope = N_HEADS * QK_ROPE
    qt = lax.dot_general(wuqt_ref[...], c_q, (((1,), (1,)), ((), ())), preferred_element_type=F32)
    cct = jnp.concatenate([cct_ref[0]] * N_HEADS, axis=0)
    sst = jnp.concatenate([sst_ref[0]] * N_HEADS, axis=0)
    q_nope = qt[0:n_nope] * scale
    qr = qt[n_nope:n_nope + n_rope]
    half = QK_ROPE // 2
    qr_swapped = jnp.concatenate(
        [qr[hd * QK_ROPE + off:hd * QK_ROPE + off + half] for hd in range(N_HEADS) for off in (half, 0)],
        axis=0)
    q_rope = (qr * cct + qr_swapped * sst) * scale
    for hd in range(N_HEADS):
        qt_ref[0, hd, 0:QK_NOPE, :] = q_nope[hd * QK_NOPE:(hd + 1) * QK_NOPE].astype(BF16)
        qt_ref[0, hd, QK_NOPE:QK_HEAD, :] = q_rope[hd * QK_ROPE:(hd + 1) * QK_ROPE].astype(BF16)
        qt_ref[0, hd, QK_HEAD:QK_PAD, :] = jnp.zeros((QK_PAD - QK_HEAD, qt.shape[1]), BF16)


def _q_proj(x, mods_l, g, wdq, qg, wuqt, cct, sst):
    b, s, d = x.shape
    ts = PROJ_TILE
    const2 = lambda i, j: (0, 0)
    full = lambda a: pl.BlockSpec(a.shape, const2)
    tab_t = pl.BlockSpec((1, QK_ROPE, ts), lambda i, j: (i, 0, j))
    return pl.pallas_call(
        _q_kernel,
        grid=(b, s // ts),
        in_specs=[pl.BlockSpec((1, ts, d), lambda i, j: (i, j, 0)),
                  pl.BlockSpec((1, N_MOD, d), lambda i, j: (i, 0, 0)),
                  full(g), full(wdq), full(qg), full(wuqt), tab_t, tab_t],
        out_specs=pl.BlockSpec((1, N_HEADS, QK_PAD, ts), lambda i, j: (i, 0, 0, j)),
        out_shape=jax.ShapeDtypeStruct((b, N_HEADS, QK_PAD, s), BF16),
        compiler_params=_params(("parallel", "parallel")),
        name="q_proj",
    )(x, mods_l, g, wdq, qg, wuqt, cct, sst)


def _attn_kernel(qt_ref, k_ref, vt_ref, o_ref, acc_ref, m_ref, s0_ref, s1_ref, s2_ref,
                 p0_ref, p1_ref):
    s_len = qt_ref.shape[3]
    tk, qb = s0_ref.shape
    s_refs = (s0_ref, s1_ref, s2_ref)
    p_refs = (p0_ref, p1_ref)

    units = []
    for d in range(s_len // tk):
        c0 = d * tk
        while c0 < s_len:
            c1 = min((c0 // qb + 1) * qb, s_len)
            units.append((d, c0, c1))
            c0 = c1

    def scores(u):
        d, c0, c1 = units[u]
        st = _dot(k_ref[0, 0, d * tk:(d + 1) * tk, :], qt_ref[0, 0, :, c0:c1])
        w_diag = min(c1, (d + 1) * tk) - c0
        if w_diag > 0:
            key = lax.broadcasted_iota(jnp.int32, (tk, w_diag), 0) + d * tk
            qry = lax.broadcasted_iota(jnp.int32, (tk, w_diag), 1) + c0
            tri = jnp.where(key <= qry, st[:, :w_diag], jnp.finfo(F32).min)
            st = tri if w_diag == c1 - c0 else jnp.concatenate([tri, st[:, w_diag:]], axis=1)
        s_refs[u % 3][:, :c1 - c0] = st
        return jnp.max(st, axis=0, keepdims=True)

    def softmax_step(u, st_max):
        d, c0, c1 = units[u]
        m_prev = m_ref[:, c0:c1]
        m_new = jnp.maximum(m_prev, st_max)
        alpha = jnp.exp2(m_prev - m_new)
        p = jnp.exp2(s_refs[u % 3][:, :c1 - c0] - m_new)
        p_refs[u % 2][:, :c1 - c0] = p.astype(BF16)
        m_ref[:, c0:c1] = m_new
        return alpha

    def accumulate(u, alpha):
        d, c0, c1 = units[u]
        pv = _dot(vt_ref[0, 0, :, d * tk:(d + 1) * tk], p_refs[u % 2][:, :c1 - c0])
        if d == 0:
            acc_ref[:, c0:c1] = pv
        else:
            acc_ref[:, c0:c1] = alpha * acc_ref[:, c0:c1] + pv

    m_ref[...] = jnp.full(m_ref.shape, -jnp.inf, F32)
    ahead = ATTN_SCORES_AHEAD
    maxes = {u: scores(u) for u in range(ahead)}
    alpha_prev = None
    for u in range(len(units)):
        if u + ahead < len(units):
            maxes[u + ahead] = scores(u + ahead)
        alpha = softmax_step(u, maxes.pop(u))
        if u:
            accumulate(u - 1, alpha_prev)
        alpha_prev = alpha
    accumulate(len(units) - 1, alpha_prev)
    o_ref[0, 0] = (acc_ref[0:V_HEAD, :] / acc_ref[V_HEAD:V_HEAD + 1, :]).T.astype(o_ref.dtype)


def _attention(qt, k, vt):
    b, nh, dk, s = qt.shape
    dv = vt.shape[2]
    assert dv == V_EXT
    tk = ATTN_KEY_CHUNK
    qb = ATTN_QUERY_BLOCK
    assert s % qb == 0 and s % tk == 0
    return pl.pallas_call(
        _attn_kernel,
        grid=(b, nh),
        in_specs=[pl.BlockSpec((1, 1, dk, s), lambda i, h: (i, h, 0, 0)),
                  pl.BlockSpec((1, 1, s, dk), lambda i, h: (i, h, 0, 0)),
                  pl.BlockSpec((1, 1, dv, s), lambda i, h: (i, h, 0, 0))],
        out_specs=pl.BlockSpec((1, 1, s, V_HEAD), lambda i, h: (i, h, 0, 0)),
        out_shape=jax.ShapeDtypeStruct((b, nh, s, V_HEAD), BF16),
        scratch_shapes=[pltpu.VMEM((dv, s), F32), pltpu.VMEM((1, s), F32),
                        *[pltpu.VMEM((tk, qb), F32)] * 3, *[pltpu.VMEM((tk, qb), BF16)] * 2],
        compiler_params=_params(("parallel", "parallel")),
        name="causal_attention",
    )(qt, k, vt)


def _prep_wuq(w_uq):
    r = w_uq.shape[0]
    w = w_uq.reshape(r, N_HEADS, QK_HEAD)
    nope = w[:, :, :QK_NOPE].reshape(r, N_HEADS * QK_NOPE)
    rope = w[:, :, QK_NOPE:].reshape(r, N_HEADS * QK_ROPE)
    return jnp.concatenate([nope, rope], axis=-1).T.astype(BF16)


def kernel(x, c, positions, mod_w, mod_b, norm1_g, norm2_g, pool_w, pool_b, pool_scale, kv_in_g,
           w_dkv, ckv_norm_g, w_uk, w_uv, w_dq, q_norm_g, w_uq, w_o, w_up, conv_w, conv_b, w_down,
           final_g):
    b, s, d = x.shape
    depth = mod_w.shape[0]
    n_pool = pool_w.shape[0]
    kv_rank = ckv_norm_g.shape[0]

    cc, ss, cct, sst = _rope_tables(positions)
    mods = _mods(c, mod_w, mod_b).reshape(depth, b, N_MOD, d)
    row = lambda a: a.reshape(1, -1)

    w_up_b = w_up.astype(BF16)
    w_down_b = w_down.astype(BF16)
    k = vt = None
    for l in range(depth):
        ffn_w = (w_up_b, conv_w[l], row(conv_b[l]), w_down_b, l)
        final = row(final_g) if l == depth - 1 else None
        if l < n_pool:
            pool = (row(norm1_g[l]), pool_w[l].astype(BF16), row(pool_b[l]), row(pool_scale[l]))
            x = _layer_tail(x, mods[l], row(norm2_g[l]), *ffn_w, pool=pool, final_g=final)
        else:
            j = l - n_pool
            q = _q_proj(x, mods[l], row(norm1_g[l]), w_dq[j].astype(BF16), row(q_norm_g[j]),
                        _prep_wuq(w_uq[j]), cct, sst)
            o = _attention(q, k, vt)
            x = _layer_tail(x, mods[l], row(norm2_g[l]), *ffn_w, attn=(o, w_o[j].astype(BF16)),
                            final_g=final)
        if l == n_pool - 1:
            wdkv = jnp.concatenate([w_dkv, w_dkv[:, kv_rank:]], axis=1).astype(BF16)
            k, vt = _shared_kv(x, row(kv_in_g), wdkv, row(ckv_norm_g), w_uk.astype(BF16),
                               w_uv.T.astype(BF16), cc, ss)
    return x
```

```python
import functools
import math

import jax
import jax.numpy as jnp
from jax import lax
from jax.experimental import pallas as pl
from jax.experimental.pallas import tpu as pltpu

F32 = jnp.float32
BF16 = jnp.bfloat16

POOL_WINDOWS = (2, 4, 8, 16)
N_HEADS = 8
QK_NOPE = 128
QK_ROPE = 64
V_HEAD = 128
QK_HEAD = QK_NOPE + QK_ROPE
QK_PAD = 256
V_EXT = V_HEAD + 16
ROPE_THETA = 10000.0
CONV_WIDTH = 3
EPS = 1e-6
N_MOD = 6

LANES = 128
SUBLANES = 8
VMEM_LIMIT_BYTES = 56 * 1024 * 1024

SEQ_TILE = 512
LAYER_SUBTILES = 2
PROJ_TILE = 1024
ROPE_TILE = 2048
FF_CHUNK = 256
ATTN_KEY_CHUNK = 256
ATTN_QUERY_BLOCK = 512
ATTN_SCORES_AHEAD = 1
POOL_HALO = 16
MODS_N_TILE = 1536


def _params(sem):
    return pltpu.CompilerParams(dimension_semantics=sem, vmem_limit_bytes=VMEM_LIMIT_BYTES)


def _rmsnorm(x, g):
    ms = jnp.mean(x * x, axis=-1, keepdims=True)
    return x * lax.rsqrt(ms + EPS) * g


def _modulated_norm(x, g, shift, scale):
    ms = jnp.mean(x * x, axis=-1, keepdims=True)
    return x * lax.rsqrt(ms + EPS) * (g * (1.0 + scale)) + shift


def _dot(a, b):
    return jnp.dot(a, b, preferred_element_type=F32)


def _rope_table_kernel(pos_ref, inv_ref, cc_ref, ss_ref, cct_ref, sst_ref):
    ang = inv_ref[...] * pos_ref[0]
    cos_t = jnp.cos(ang)
    sin_t = jnp.sin(ang)
    cct = jnp.concatenate([cos_t, cos_t], axis=0)
    sst = jnp.concatenate([-sin_t, sin_t], axis=0)
    cct_ref[0] = cct
    sst_ref[0] = sst
    cc_ref[0] = jnp.concatenate([cct, cct], axis=0).T
    ss_ref[0] = jnp.concatenate([sst, sst], axis=0).T


def _rope_tables(positions):
    b, s = positions.shape
    half = QK_ROPE // 2
    inv = 1.0 / (ROPE_THETA ** (jnp.arange(0, QK_ROPE, 2, dtype=F32) / QK_ROPE))
    pos = positions.astype(F32)[:, None, :]
    ts = ROPE_TILE
    tab = pl.BlockSpec((1, ts, LANES), lambda i, j: (i, j, 0))
    tab_t = pl.BlockSpec((1, QK_ROPE, ts), lambda i, j: (i, 0, j))
    return pl.pallas_call(
        _rope_table_kernel,
        grid=(b, s // ts),
        in_specs=[pl.BlockSpec((1, 1, ts), lambda i, j: (i, 0, j)),
                  pl.BlockSpec((half, 1), lambda i, j: (0, 0))],
        out_specs=[tab, tab, tab_t, tab_t],
        out_shape=[jax.ShapeDtypeStruct((b, s, LANES), F32)] * 2
                  + [jax.ShapeDtypeStruct((b, QK_ROPE, s), F32)] * 2,
        compiler_params=_params(("parallel", "parallel")),
        name="rope_tables",
    )(pos, inv[:, None])


def _mods_kernel(c_ref, w_ref, b_ref, o_ref):
    c = c_ref[...]
    sc = (c * jax.nn.sigmoid(c)).astype(BF16)
    o_ref[0] = _dot(sc, w_ref[0].astype(BF16)) + b_ref[0]


def _mods(c, mod_w, mod_b):
    depth, d, n = mod_w.shape
    b = c.shape[0]
    nt = MODS_N_TILE
    return pl.pallas_call(
        _mods_kernel,
        grid=(depth, n // nt),
        in_specs=[pl.BlockSpec((b, d), lambda l, j: (0, 0)),
                  pl.BlockSpec((1, d, nt), lambda l, j: (l, 0, j)),
                  pl.BlockSpec((1, 1, nt), lambda l, j: (l, 0, j))],
        out_specs=pl.BlockSpec((1, b, nt), lambda l, j: (l, 0, j)),
        out_shape=jax.ShapeDtypeStruct((depth, b, n), F32),
        compiler_params=_params(("parallel", "parallel")),
        name="adaln_mods",
    )(c, mod_w, mod_b.reshape(depth, 1, n))


def _shift_rows(a, k, prev_rows):
    rolled = pltpu.roll(a, k, axis=0)
    prev = pltpu.roll(prev_rows, k, axis=0)
    row = lax.broadcasted_iota(jnp.int32, prev_rows.shape, 0)
    head = jnp.where(row < k, prev, rolled[0:SUBLANES])
    return jnp.concatenate([head, rolled[SUBLANES:]], axis=0)


def _pooled_groups(h, halo, first_pos):
    rows, d = h.shape
    group = d // len(POOL_WINDOWS)
    t = first_pos + lax.broadcasted_iota(jnp.int32, (rows, 1), 0)
    pooled = []
    for gi, w in enumerate(POOL_WINDOWS):
        lo = gi * group
        hg = h[:, lo:lo + group]
        win = jnp.concatenate([halo[:, lo:lo + group], hg], axis=0)
        k = 1
        while k < w:
            win = win + pltpu.roll(win, k, axis=0)
            k *= 2
        cnt = jnp.minimum(t + 1, w).astype(F32)
        pooled.append((win[POOL_HALO:] / cnt - hg).astype(BF16))
    return pooled


def _layer_kernel(*refs, mixer, with_final):
    it = iter(refs)
    x_ref = next(it)
    mod_ref = next(it)
    if mixer == "attn":
        o_in_ref = next(it)
        wo_ref = next(it)
    else:
        g1_ref = next(it)
        pw_ref = next(it)
        pb_ref = next(it)
        ps_ref = next(it)
    g2_ref = next(it)
    wup_ref = next(it)
    cw_ref = next(it)
    cb_ref = next(it)
    wd_ref = next(it)
    if with_final:
        fg_ref = next(it)
    out_ref = next(it)
    carry_ref = next(it)
    gate_ref = next(it)
    if mixer == "pool":
        halo_ref = next(it)

    s = pl.program_id(1)
    ts = x_ref.shape[1]
    sub = ts // LAYER_SUBTILES
    f = wd_ref.shape[0]
    fc = FF_CHUNK
    n_chunks = f // fc

    @pl.when(s == 0)
    def _():
        carry_ref[...] = jnp.zeros(carry_ref.shape, F32)
        if mixer == "pool":
            halo_ref[...] = jnp.zeros(halo_ref.shape, F32)

    mod = mod_ref[0]

    halo = halo_ref[...] if mixer == "pool" else None

    def mixer_vector_part(t, anchor=None):
        nonlocal halo
        if mixer == "attn":
            return None
        x = x_ref[0, t * sub:(t + 1) * sub]
        if anchor is not None:
            x = x + anchor
        h1 = _modulated_norm(x, g1_ref[...], mod[0:1], mod[1:2])
        pooled = _pooled_groups(h1, halo, s * ts + t * sub)
        halo = h1[sub - POOL_HALO:sub]
        return pooled

    def mixer_matmul_part(t, pooled):
        rows = slice(t * sub, (t + 1) * sub)
        if mixer == "attn":
            o = jnp.concatenate([o_in_ref[0, hd, rows, :] for hd in range(N_HEADS)], axis=-1)
            y = _dot(o, wo_ref[...])
        else:
            y = jnp.concatenate([_dot(p, pw_ref[gi]) for gi, p in enumerate(pooled)], axis=-1)
            y = (y + pb_ref[...]) * ps_ref[...]
        x1 = x_ref[0, rows] + mod[2:3] * y
        out_ref[0, rows] = x1
        return _modulated_norm(x1, g2_ref[...], mod[3:4], mod[4:5]).astype(BF16)

    def zero_after(value):
        bits = lax.bitcast_convert_type(value[0:1, 0:LANES], jnp.uint32)
        bits = lax.shift_right_logical(lax.shift_right_logical(bits, jnp.uint32(16)), jnp.uint32(16))
        zeros = lax.bitcast_convert_type(bits, F32)
        return jnp.concatenate([zeros] * (x_ref.shape[2] // LANES), axis=1)

    sqrt_half = math.sqrt(0.5)
    tails = [carry_ref[:, j * fc:(j + 1) * fc] for j in range(n_chunks)]
    h_next = mixer_matmul_part(0, mixer_vector_part(0))
    for t in range(LAYER_SUBTILES):
        rows = slice(t * sub, (t + 1) * sub)
        h = h_next
        up = lambda j: (_dot(h, wup_ref[:, j * fc:(j + 1) * fc]),
                        _dot(h, wup_ref[:, f + j * fc:f + (j + 1) * fc]))
        nxt = up(0)
        if t + 1 < LAYER_SUBTILES:
            pooled_next = mixer_vector_part(t + 1, anchor=zero_after(nxt[0]))
        for j in range(n_chunks):
            cols = slice(j * fc, (j + 1) * fc)
            a, v = nxt
            if j + 1 < n_chunks:
                nxt = up(j + 1)
            a1 = _shift_rows(a, 1, tails[j])
            a2 = _shift_rows(a, 2, tails[j])
            conv = (a2 * cw_ref[0:1, cols] + a1 * cw_ref[1:2, cols] + a * cw_ref[2:3, cols]
                    + cb_ref[:, cols])
            gate = 0.5 * conv * (1.0 + lax.erf(conv * sqrt_half))
            gate_ref[:, cols] = (gate * v).astype(BF16)
            tails[j] = a[sub - SUBLANES:sub]
        if t + 1 < LAYER_SUBTILES:
            h_next = mixer_matmul_part(t + 1, pooled_next)
        y = out_ref[0, rows] + mod[5:6] * _dot(gate_ref[...], wd_ref[...])
        if with_final:
            y = _rmsnorm(y, fg_ref[...])
        out_ref[0, rows] = y
    for j in range(n_chunks):
        carry_ref[:, j * fc:(j + 1) * fc] = tails[j]
    if mixer == "pool":
        halo_ref[...] = halo


def _layer_tail(x, mods_l, g2, wup_all, cw, cb, wd_all, layer, attn=None, pool=None, final_g=None):
    b, s, d = x.shape
    ts = LAYER_SUBTILES * SEQ_TILE
    f = wd_all.shape[1]
    assert f % FF_CHUNK == 0 and s % ts == 0 and (attn is None) != (pool is None)
    layer_block = lambda i, j: (layer, 0, 0)
    const2 = lambda i, j: (0, 0)
    vec = pl.BlockSpec((1, d), const2)
    tile = pl.BlockSpec((1, ts, d), lambda i, j: (i, j, 0))
    single = dict(pipeline_mode=pl.Buffered(1))
    in_specs = [tile, pl.BlockSpec((1, N_MOD, d), lambda i, j: (i, 0, 0))]
    args = [x, mods_l]
    scratch = [pltpu.VMEM((SUBLANES, f), F32), pltpu.VMEM((SEQ_TILE, f), BF16)]
    if attn is not None:
        o, wo = attn
        in_specs += [pl.BlockSpec((1, o.shape[1], ts, o.shape[3]), lambda i, j: (i, 0, j, 0)),
                     pl.BlockSpec(wo.shape, const2, **single)]
        args += [o, wo]
    else:
        g1, pw, pb, ps = pool
        in_specs += [vec, pl.BlockSpec(pw.shape, lambda i, j: (0, 0, 0)), vec, vec]
        args += [g1, pw, pb, ps]
        scratch.append(pltpu.VMEM((POOL_HALO, d), F32))
    in_specs += [vec,
                 pl.BlockSpec((None,) + wup_all.shape[1:], layer_block, **single),
                 pl.BlockSpec(cw.shape, const2),
                 pl.BlockSpec(cb.shape, const2),
                 pl.BlockSpec((None,) + wd_all.shape[1:], layer_block, **single)]
    args += [g2, wup_all, cw, cb, wd_all]
    if final_g is not None:
        in_specs.append(vec)
        args.append(final_g)
    kern = functools.partial(_layer_kernel, mixer="attn" if attn is not None else "pool",
                             with_final=final_g is not None)
    return pl.pallas_call(
        kern,
        grid=(b, s // ts),
        in_specs=in_specs,
        out_specs=tile,
        out_shape=jax.ShapeDtypeStruct(x.shape, F32),
        scratch_shapes=scratch,
        compiler_params=_params(("arbitrary", "arbitrary")),
        name="layer_tail_" + ("attn" if attn is not None else "pool"),
    )(*args)


def _kv_kernel(x_ref, g_ref, wdkv_ref, cg_ref, wuk_ref, wuvt_ref, cc_ref, ss_ref, k_ref, vt_ref):
    ts = x_ref.shape[1]
    kv_rank = cg_ref.shape[1]
    xn = _rmsnorm(x_ref[0], g_ref[...]).astype(BF16)
    kv = _dot(xn, wdkv_ref[...])
    c_kv = _rmsnorm(kv[:, 0:kv_rank], cg_ref[...]).astype(BF16)
    kr = kv[:, kv_rank:kv_rank + LANES]
    k_rope = kr * cc_ref[0] + pltpu.roll(kr, QK_ROPE // 2, axis=1) * ss_ref[0]
    k_rope = k_rope[:, 0:QK_ROPE].astype(BF16)
    row = lax.broadcasted_iota(jnp.int32, (V_EXT - V_HEAD, ts), 0)
    ones_row = jnp.where(row == 0, 1.0, 0.0).astype(BF16)
    k_nope = _dot(c_kv, wuk_ref[...])
    vt = lax.dot_general(wuvt_ref[...], c_kv, (((1,), (1,)), ((), ())), preferred_element_type=F32)
    for hd in range(N_HEADS):
        k_ref[0, hd, :, 0:QK_NOPE] = k_nope[:, hd * QK_NOPE:(hd + 1) * QK_NOPE].astype(BF16)
        k_ref[0, hd, :, QK_NOPE:QK_HEAD] = k_rope
        k_ref[0, hd, :, QK_HEAD:QK_PAD] = jnp.zeros((ts, QK_PAD - QK_HEAD), BF16)
        vt_ref[0, hd, 0:V_HEAD, :] = vt[hd * V_HEAD:(hd + 1) * V_HEAD, :].astype(BF16)
        vt_ref[0, hd, V_HEAD:V_EXT, :] = ones_row


def _shared_kv(x, g, wdkv, cg, wuk, wuvt, cc, ss):
    b, s, d = x.shape
    ts = PROJ_TILE
    const2 = lambda i, j: (0, 0)
    full = lambda a: pl.BlockSpec(a.shape, const2)
    tab = pl.BlockSpec((1, ts, LANES), lambda i, j: (i, j, 0))
    return pl.pallas_call(
        _kv_kernel,
        grid=(b, s // ts),
        in_specs=[pl.BlockSpec((1, ts, d), lambda i, j: (i, j, 0)),
                  full(g), full(wdkv), full(cg), full(wuk), full(wuvt), tab, tab],
        out_specs=[pl.BlockSpec((1, N_HEADS, ts, QK_PAD), lambda i, j: (i, 0, j, 0)),
                   pl.BlockSpec((1, N_HEADS, V_EXT, ts), lambda i, j: (i, 0, 0, j))],
        out_shape=[jax.ShapeDtypeStruct((b, N_HEADS, s, QK_PAD), BF16),
                   jax.ShapeDtypeStruct((b, N_HEADS, V_EXT, s), BF16)],
        compiler_params=_params(("parallel", "parallel")),
        name="shared_kv",
    )(x, g, wdkv, cg, wuk, wuvt, cc, ss)


def _q_kernel(x_ref, mod_ref, g_ref, wdq_ref, qg_ref, wuqt_ref, cct_ref, sst_ref, qt_ref):
    mod = mod_ref[0]
    h = _modulated_norm(x_ref[0], g_ref[...], mod[0:1], mod[1:2]).astype(BF16)
    c_q = _rmsnorm(_dot(h, wdq_ref[...]), qg_ref[...]).astype(BF16)
    scale = QK_HEAD ** -0.5 * math.log2(math.e)
    n_nope = N_HEADS * QK_NOPE
    n_rope = N_HEADS * QK_ROPE
    qt = lax.dot_general(wuqt_ref[...], c_q, (((1,), (1,)), ((), ())), preferred_element_type=F32)
    cct = jnp.concatenate([cct_ref[0]] * N_HEADS, axis=0)
    sst = jnp.concatenate([sst_ref[0]] * N_HEADS, axis=0)
    q_nope = qt[0:n_nope] * scale
    qr = qt[n_nope:n_nope + n_rope]
    half = QK_ROPE // 2
    qr_swapped = jnp.concatenate(
        [qr[hd * QK_ROPE + off:hd * QK_ROPE + off + half] for hd in range(N_HEADS) for off in (half, 0)],
        axis=0)
    q_rope = (qr * cct + qr_swapped * sst) * scale
    for hd in range(N_HEADS):
        qt_ref[0, hd, 0:QK_NOPE, :] = q_nope[hd * QK_NOPE:(hd + 1) * QK_NOPE].astype(BF16)
        qt_ref[0, hd, QK_NOPE:QK_HEAD, :] = q_rope[hd * QK_ROPE:(hd + 1) * QK_ROPE].astype(BF16)
        qt_ref[0, hd, QK_HEAD:QK_PAD, :] = jnp.zeros((QK_PAD - QK_HEAD, qt.shape[1]), BF16)


def _q_proj(x, mods_l, g, wdq, qg, wuqt, cct, sst):
    b, s, d = x.shape
    ts = PROJ_TILE
    const2 = lambda i, j: (0, 0)
    full = lambda a: pl.BlockSpec(a.shape, const2)
    tab_t = pl.BlockSpec((1, QK_ROPE, ts), lambda i, j: (i, 0, j))
    return pl.pallas_call(
        _q_kernel,
        grid=(b, s // ts),
        in_specs=[pl.BlockSpec((1, ts, d), lambda i, j: (i, j, 0)),
                  pl.BlockSpec((1, N_MOD, d), lambda i, j: (i, 0, 0)),
                  full(g), full(wdq), full(qg), full(wuqt), tab_t, tab_t],
        out_specs=pl.BlockSpec((1, N_HEADS, QK_PAD, ts), lambda i, j: (i, 0, 0, j)),
        out_shape=jax.ShapeDtypeStruct((b, N_HEADS, QK_PAD, s), BF16),
        compiler_params=_params(("parallel", "parallel")),
        name="q_proj",
    )(x, mods_l, g, wdq, qg, wuqt, cct, sst)


def _attn_kernel(qt_ref, k_ref, vt_ref, o_ref, acc_ref, m_ref, s0_ref, s1_ref, s2_ref,
                 p0_ref, p1_ref):
    s_len = qt_ref.shape[3]
    tk, qb = s0_ref.shape
    s_refs = (s0_ref, s1_ref, s2_ref)
    p_refs = (p0_ref, p1_ref)

    units = []
    for d in range(s_len // tk):
        c0 = d * tk
        while c0 < s_len:
            c1 = min((c0 // qb + 1) * qb, s_len)
            units.append((d, c0, c1))
            c0 = c1

    def scores(u):
        d, c0, c1 = units[u]
        st = _dot(k_ref[0, 0, d * tk:(d + 1) * tk, :], qt_ref[0, 0, :, c0:c1])
        w_diag = min(c1, (d + 1) * tk) - c0
        if w_diag > 0:
            key = lax.broadcasted_iota(jnp.int32, (tk, w_diag), 0) + d * tk
            qry = lax.broadcasted_iota(jnp.int32, (tk, w_diag), 1) + c0
            tri = jnp.where(key <= qry, st[:, :w_diag], jnp.finfo(F32).min)
            st = tri if w_diag == c1 - c0 else jnp.concatenate([tri, st[:, w_diag:]], axis=1)
        s_refs[u % 3][:, :c1 - c0] = st
        return jnp.max(st, axis=0, keepdims=True)

    def softmax_step(u, st_max):
        d, c0, c1 = units[u]
        m_prev = m_ref[:, c0:c1]
        m_new = jnp.maximum(m_prev, st_max)
        alpha = jnp.exp2(m_prev - m_new)
        p = jnp.exp2(s_refs[u % 3][:, :c1 - c0] - m_new)
        p_refs[u % 2][:, :c1 - c0] = p.astype(BF16)
        m_ref[:, c0:c1] = m_new
        return alpha

    def accumulate(u, alpha):
        d, c0, c1 = units[u]
        pv = _dot(vt_ref[0, 0, :, d * tk:(d + 1) * tk], p_refs[u % 2][:, :c1 - c0])
        if d == 0:
            acc_ref[:, c0:c1] = pv
        else:
            acc_ref[:, c0:c1] = alpha * acc_ref[:, c0:c1] + pv

    m_ref[...] = jnp.full(m_ref.shape, -jnp.inf, F32)
    ahead = ATTN_SCORES_AHEAD
    maxes = {u: scores(u) for u in range(ahead)}
    alpha_prev = None
    for u in range(len(units)):
        if u + ahead < len(units):
            maxes[u + ahead] = scores(u + ahead)
        alpha = softmax_step(u, maxes.pop(u))
        if u:
            accumulate(u - 1, alpha_prev)
        alpha_prev = alpha
    accumulate(len(units) - 1, alpha_prev)
    o_ref[0, 0] = (acc_ref[0:V_HEAD, :] / acc_ref[V_HEAD:V_HEAD + 1, :]).T.astype(o_ref.dtype)


def _attention(qt, k, vt):
    b, nh, dk, s = qt.shape
    dv = vt.shape[2]
    assert dv == V_EXT
    tk = ATTN_KEY_CHUNK
    qb = ATTN_QUERY_BLOCK
    assert s % qb == 0 and s % tk == 0
    return pl.pallas_call(
        _attn_kernel,
        grid=(b, nh),
        in_specs=[pl.BlockSpec((1, 1, dk, s), lambda i, h: (i, h, 0, 0)),
                  pl.BlockSpec((1, 1, s, dk), lambda i, h: (i, h, 0, 0)),
                  pl.BlockSpec((1, 1, dv, s), lambda i, h: (i, h, 0, 0))],
        out_specs=pl.BlockSpec((1, 1, s, V_HEAD), lambda i, h: (i, h, 0, 0)),
        out_shape=jax.ShapeDtypeStruct((b, nh, s, V_HEAD), BF16),
        scratch_shapes=[pltpu.VMEM((dv, s), F32), pltpu.VMEM((1, s), F32),
                        *[pltpu.VMEM((tk, qb), F32)] * 3, *[pltpu.VMEM((tk, qb), BF16)] * 2],
        compiler_params=_params(("parallel", "parallel")),
        name="causal_attention",
    )(qt, k, vt)


def _prep_wuq(w_uq):
    r = w_uq.shape[0]
    w = w_uq.reshape(r, N_HEADS, QK_HEAD)
    nope = w[:, :, :QK_NOPE].reshape(r, N_HEADS * QK_NOPE)
    rope = w[:, :, QK_NOPE:].reshape(r, N_HEADS * QK_ROPE)
    return jnp.concatenate([nope, rope], axis=-1).T.astype(BF16)


def kernel(x, c, positions, mod_w, mod_b, norm1_g, norm2_g, pool_w, pool_b, pool_scale, kv_in_g,
           w_dkv, ckv_norm_g, w_uk, w_uv, w_dq, q_norm_g, w_uq, w_o, w_up, conv_w, conv_b, w_down,
           final_g):
    b, s, d = x.shape
    depth = mod_w.shape[0]
    n_pool = pool_w.shape[0]
    kv_rank = ckv_norm_g.shape[0]

    cc, ss, cct, sst = _rope_tables(positions)
    mods = _mods(c, mod_w, mod_b).reshape(depth, b, N_MOD, d)
    row = lambda a: a.reshape(1, -1)

    w_up_b = w_up.astype(BF16)
    w_down_b = w_down.astype(BF16)
    k = vt = None
    for l in range(depth):
        ffn_w = (w_up_b, conv_w[l], row(conv_b[l]), w_down_b, l)
        final = row(final_g) if l == depth - 1 else None
        if l < n_pool:
            pool = (row(norm1_g[l]), pool_w[l].astype(BF16), row(pool_b[l]), row(pool_scale[l]))
            x = _layer_tail(x, mods[l], row(norm2_g[l]), *ffn_w, pool=pool, final_g=final)
        else:
            j = l - n_pool
            q = _q_proj(x, mods[l], row(norm1_g[l]), w_dq[j].astype(BF16), row(q_norm_g[j]),
                        _prep_wuq(w_uq[j]), cct, sst)
            o = _attention(q, k, vt)
            x = _layer_tail(x, mods[l], row(norm2_g[l]), *ffn_w, attn=(o, w_o[j].astype(BF16)),
                            final_g=final)
        if l == n_pool - 1:
            wdkv = jnp.concatenate([w_dkv, w_dkv[:, kv_rank:]], axis=1).astype(BF16)
            k, vt = _shared_kv(x, row(kv_in_g), wdkv, row(ckv_norm_g), w_uk.astype(BF16),
                               w_uv.T.astype(BF16), cc, ss)
    return x
```
